```python
import functools
import jax, jax.numpy as jnp
from jax import lax
import numpy as np

D_MODEL = 1024
BATCH = 4
SEQ = 4096
DEPTH = 1
DEC_BATCH = 32
DEC_SEQ = 1
PAST_LEN = 8192
PAGE_SIZE = 128

D_RNN = D_MODEL
N_BLK = 16
BLK_W = D_RNN // N_BLK
CONV_W = 4
RG_C = 8.0
N_HEADS = 16
HEAD_DIM = D_MODEL // N_HEADS
ATTN_W = N_HEADS * HEAD_DIM
Q_BLK = 128
D_FF = ((8 * D_MODEL + 3 * 256 - 1) // (3 * 256)) * 256
N_IN = 2 * D_RNN + 3 * ATTN_W + N_HEADS + 2 * D_MODEL
EPS = 1e-6

kernel_name = "hawk_fox_adaln_decode_step"


def rms_norm(x, g):
    xf = x.astype(jnp.float32)
    y = xf * lax.rsqrt(jnp.mean(xf * xf, axis=-1, keepdims=True) + EPS)
    return (y * g.astype(jnp.float32)).astype(x.dtype)


def ada_modulation(c, ada_w, ada_b):
    m = (jax.nn.silu(c) @ ada_w + ada_b)[:, None, :]
    return jnp.split(m, 6, axis=-1)


def causal_dwconv(x, buf, w, b):
    xp = jnp.concatenate([buf.astype(x.dtype), x], axis=1)
    T = x.shape[1]
    y = b
    for j in range(CONV_W):
        y = y + xp[:, j:j + T] * w[j]
    return y, xp[:, T:]


def rg_lru(x, h0, pos0, wr, br, wi, bi, lam):
    B, T, _ = x.shape
    f32 = jnp.float32
    xf = x.astype(f32)
    xb = xf.reshape(B, T, N_BLK, BLK_W)
    gate_r = jax.nn.sigmoid(jnp.einsum("btnc,ncd->btnd", xb, wr.astype(f32)).reshape(B, T, D_RNN) + br.astype(f32))
    gate_i = jax.nn.sigmoid(jnp.einsum("btnc,ncd->btnd", xb, wi.astype(f32)).reshape(B, T, D_RNN) + bi.astype(f32))
    log_a = -RG_C * gate_r * jax.nn.softplus(-lam.astype(f32))
    a = jnp.exp(log_a)
    mult = jnp.sqrt(-jnp.expm1(2.0 * log_a))
    pos = pos0 + jnp.arange(T)
    mult = jnp.where((pos == 0)[None, :, None], 1.0, mult)
    u = mult * gate_i * xf

    def combine(lhs, rhs):
        a1, u1 = lhs
        a2, u2 = rhs
        return a1 * a2, a2 * u1 + u2

    a_cum, u_cum = lax.associative_scan(combine, (a, u), axis=1)
    h = a_cum * h0.astype(f32)[:, None, :] + u_cum
    return h, h[:, -1]


def fox_prompt(q, k, v, logf):
    B, S = q.shape[:2]
    n_blk = S // Q_BLK
    scale = HEAD_DIM ** -0.5
    Fk = jnp.transpose(jnp.cumsum(logf, axis=1), (0, 2, 1))
    kpos = jnp.arange(S)
    qb = jnp.moveaxis(q.reshape(B, n_blk, Q_BLK, N_HEADS, HEAD_DIM), 1, 0)
    Fq = jnp.moveaxis(Fk.reshape(B, N_HEADS, n_blk, Q_BLK), 2, 0)

    def one_block(args):
        q_i, fq_i, blk = args
        s = jnp.einsum("bqhd,bkhd->bhqk", q_i, k, preferred_element_type=jnp.float32) * scale
        s = s + fq_i[..., None] - Fk[:, :, None, :]
        qpos = blk * Q_BLK + jnp.arange(Q_BLK)
        s = jnp.where(kpos[None, :] <= qpos[:, None], s, -jnp.inf)
        p = jax.nn.softmax(s, axis=-1)
        return jnp.einsum("bhqk,bkhd->bqhd", p.astype(v.dtype), v)

    o = lax.map(one_block, (qb, Fq, jnp.arange(n_blk)))
    return jnp.moveaxis(o, 0, 1).reshape(B, S, N_HEADS, HEAD_DIM)


def fox_sample(q, k, v, logf, cache_k, cache_v, cache_logf, page_table):
    DB, T = q.shape[:2]
    P = page_table.shape[1] * PAGE_SIZE
    scale = HEAD_DIM ** -0.5
    k_all = jnp.concatenate([cache_k[page_table].reshape(DB, P, N_HEADS, HEAD_DIM).astype(k.dtype), k], axis=1)
    v_all = jnp.concatenate([cache_v[page_table].reshape(DB, P, N_HEADS, HEAD_DIM).astype(v.dtype), v], axis=1)
    lf_all = jnp.concatenate([cache_logf[page_table].reshape(DB, P, N_HEADS).astype(jnp.float32), logf], axis=1)
    Fk = jnp.transpose(jnp.cumsum(lf_all, axis=1), (0, 2, 1))
    s = jnp.einsum("bqhd,bkhd->bhqk", q, k_all, preferred_element_type=jnp.float32) * scale
    s = s + Fk[:, :, P:, None] - Fk[:, :, None, :]
    kpos = jnp.arange(P + T)
    qpos = P + jnp.arange(T)
    s = jnp.where(kpos[None, :] <= qpos[:, None], s, -jnp.inf)
    p = jax.nn.softmax(s, axis=-1)
    return jnp.einsum("bhqk,bkhd->bqhd", p.astype(v_all.dtype), v_all)


def trunk_layer(x, c, conv_buf, h0, pos0, attend, ada_w, ada_b, norm1_g, norm2_g, w_in, b_f,
                conv_w, conv_b, rg_wr, rg_br, rg_wi, rg_bi, rg_lambda, w_proj_a, w_proj_b, w_o,
                w_ffn_in, w_ffn_out):
    B, T, _ = x.shape
    sh1, sc1, g1, sh2, sc2, g2 = ada_modulation(c, ada_w, ada_b)
    h = rms_norm(x, norm1_g) * (1.0 + sc1) + sh1
    z = h @ w_in
    cuts = list(np.cumsum([D_RNN, D_RNN, ATTN_W, ATTN_W, ATTN_W, N_HEADS, D_MODEL]))
    xr, gr, q, k, v, fl, ga, gb = jnp.split(z, cuts, axis=-1)
    xc, conv_new = causal_dwconv(xr, conv_buf, conv_w, conv_b)
    hr, h_last = rg_lru(xc, h0, pos0, rg_wr, rg_br, rg_wi, rg_bi, rg_lambda)
    ya = (hr.astype(x.dtype) * jax.nn.gelu(gr)) @ w_proj_a
    q = q.reshape(B, T, N_HEADS, HEAD_DIM)
    k = k.reshape(B, T, N_HEADS, HEAD_DIM)
    v = v.reshape(B, T, N_HEADS, HEAD_DIM)
    logf = jax.nn.log_sigmoid((fl + b_f).astype(jnp.float32))
    yb = attend(q, k, v, logf).reshape(B, T, ATTN_W) @ w_proj_b
    mixed = (jax.nn.sigmoid(ga) * ya + jax.nn.sigmoid(gb) * yb) @ w_o
    x = x + g1 * mixed
    h2 = rms_norm(x, norm2_g) * (1.0 + sc2) + sh2
    gu, up = jnp.split(h2 @ w_ffn_in, 2, axis=-1)
    x = x + g2 * ((jax.nn.silu(gu) * up) @ w_ffn_out)
    return x, k, v, logf, conv_new, h_last


def setup_inputs(seed: int = 0) -> dict:
    key = jax.random.key(seed)
    ks = jax.random.split(key, 32)
    f32 = jnp.float32
    n_pages = PAST_LEN // PAGE_SIZE
    n_pool = (DEC_BATCH * n_pages * 5) // 4

    def nrm(k, shape, scale):
        return jax.random.normal(k, shape, f32) * scale

    page_table = jax.random.permutation(ks[0], n_pool)[:DEC_BATCH * n_pages].reshape(DEC_BATCH, n_pages).astype(jnp.int32)
    u = jax.random.uniform(ks[1], (DEPTH, D_RNN), f32, 0.9, 0.999)
    s = u ** (1.0 / RG_C)
    rg_lambda = jnp.log(s) - jnp.log1p(-s)
    b_f = jnp.linspace(1.0, 4.0, N_HEADS, dtype=f32)[None, :] + nrm(ks[2], (DEPTH, N_HEADS), 0.1)
    return {
        "x_prompt": nrm(ks[3], (BATCH, SEQ, D_MODEL), 1.0),
        "x_sample": nrm(ks[4], (DEC_BATCH, DEC_SEQ, D_MODEL), 1.0),
        "c_prompt": nrm(ks[5], (BATCH, D_MODEL), 1.0),
        "c_sample": nrm(ks[6], (DEC_BATCH, D_MODEL), 1.0),
        "cache_k": nrm(ks[7], (DEPTH, n_pool, PAGE_SIZE, N_HEADS, HEAD_DIM), 1.0),
        "cache_v": nrm(ks[8], (DEPTH, n_pool, PAGE_SIZE, N_HEADS, HEAD_DIM), 1.0),
        "cache_logf": jax.nn.log_sigmoid(3.0 + nrm(ks[9], (DEPTH, n_pool, PAGE_SIZE, N_HEADS), 0.5)),
        "state_conv": nrm(ks[10], (DEPTH, DEC_BATCH, CONV_W - 1, D_RNN), 1.0),
        "state_rglru": nrm(ks[11], (DEPTH, DEC_BATCH, D_RNN), 0.5),
        "page_table": page_table,
        "ada_w": nrm(ks[12], (DEPTH, D_MODEL, 6 * D_MODEL), 0.5 * D_MODEL ** -0.5),
        "ada_b": nrm(ks[13], (DEPTH, 6 * D_MODEL), 0.02),
        "norm1_g": 1.0 + nrm(ks[14], (DEPTH, D_MODEL), 0.05),
        "norm2_g": 1.0 + nrm(ks[15], (DEPTH, D_MODEL), 0.05),
        "normf_g": 1.0 + nrm(ks[16], (D_MODEL,), 0.05),
        "w_in": nrm(ks[17], (DEPTH, D_MODEL, N_IN), D_MODEL ** -0.5),
        "b_f": b_f,
        "conv_w": nrm(ks[18], (DEPTH, CONV_W, D_RNN), CONV_W ** -0.5),
        "conv_b": nrm(ks[19], (DEPTH, D_RNN), 0.02),
        "rg_wr": nrm(ks[20], (DEPTH, N_BLK, BLK_W, BLK_W), BLK_W ** -0.5),
        "rg_br": nrm(ks[21], (DEPTH, D_RNN), 0.02),
        "rg_wi": nrm(ks[22], (DEPTH, N_BLK, BLK_W, BLK_W), BLK_W ** -0.5),
        "rg_bi": nrm(ks[23], (DEPTH, D_RNN), 0.02),
        "rg_lambda": rg_lambda,
        "w_proj_a": nrm(ks[24], (DEPTH, D_RNN, D_MODEL), D_RNN ** -0.5),
        "w_proj_b": nrm(ks[25], (DEPTH, ATTN_W, D_MODEL), ATTN_W ** -0.5),
        "w_o": nrm(ks[26], (DEPTH, D_MODEL, D_MODEL), D_MODEL ** -0.5),
        "w_ffn_in": nrm(ks[27], (DEPTH, D_MODEL, 2 * D_FF), D_MODEL ** -0.5),
        "w_ffn_out": nrm(ks[28], (DEPTH, D_FF, D_MODEL), D_FF ** -0.5),
    }


def reference(x_prompt, x_sample, c_prompt, c_sample, cache_k, cache_v, cache_logf, state_conv,
              state_rglru, page_table, ada_w, ada_b, norm1_g, norm2_g, normf_g, w_in, b_f, conv_w,
              conv_b, rg_wr, rg_br, rg_wi, rg_bi, rg_lambda, w_proj_a, w_proj_b, w_o, w_ffn_in,
              w_ffn_out):
    B = x_prompt.shape[0]
    past_len = page_table.shape[1] * PAGE_SIZE
    zero_buf = jnp.zeros((B, CONV_W - 1, D_RNN), x_prompt.dtype)
    zero_h = jnp.zeros((B, D_RNN), jnp.float32)
    y_p, y_s = x_prompt, x_sample
    k_p, v_p, lf_p, cv_p, h_p = [], [], [], [], []
    k_s, v_s, lf_s, cv_s, h_s = [], [], [], [], []
    for l in range(DEPTH):
        lw = (ada_w[l], ada_b[l], norm1_g[l], norm2_g[l], w_in[l], b_f[l], conv_w[l], conv_b[l],
              rg_wr[l], rg_br[l], rg_wi[l], rg_bi[l], rg_lambda[l], w_proj_a[l], w_proj_b[l],
              w_o[l], w_ffn_in[l], w_ffn_out[l])
        y_p, kk, vv, lf, cb, hl = trunk_layer(y_p, c_prompt, zero_buf, zero_h, 0, fox_prompt, *lw)
        k_p.append(kk); v_p.append(vv); lf_p.append(lf); cv_p.append(cb); h_p.append(hl)
        attend = functools.partial(fox_sample, cache_k=cache_k[l], cache_v=cache_v[l],
                                   cache_logf=cache_logf[l], page_table=page_table)
        y_s, kk, vv, lf, cb, hl = trunk_layer(y_s, c_sample, state_conv[l], state_rglru[l],
                                              past_len, attend, *lw)
        k_s.append(kk); v_s.append(vv); lf_s.append(lf); cv_s.append(cb); h_s.append(hl)
    y_prompt = rms_norm(y_p, normf_g)
    y_sample = rms_norm(y_s, normf_g)
    return (y_prompt, y_sample,
            jnp.stack(k_p), jnp.stack(v_p), jnp.stack(lf_p), jnp.stack(cv_p), jnp.stack(h_p),
            jnp.stack(k_s), jnp.stack(v_s), jnp.stack(lf_s), jnp.stack(cv_s), jnp.stack(h_s))
```

```python
import functools

import numpy as np
import jax
import jax.numpy as jnp
from jax import lax
from jax.experimental import pallas as pl
from jax.experimental.pallas import tpu as pltpu

F32 = jnp.float32
BF16 = jnp.bfloat16

D_MODEL = 1024
N_HEADS = 16
HEAD_DIM = 64
N_BLK = 16
CONV_W = 4
RG_C = 8.0
PAGE_SIZE = 128
D_FF = 2816
EPS = 1e-6

LANES = 128
SUBLANES = 8
MXU_DIM = 256
VMEM_LIMIT = 56 * 1024 * 1024

ROW_TILE = 256
SCAN_TILE = 256
ATT_TILE = 512
PAGES_PER_STEP = 4

N_SEG = 7
W_ALL_COLS = N_SEG * D_MODEL + LANES


def _params(*sem):
    return pltpu.CompilerParams(dimension_semantics=sem, vmem_limit_bytes=VMEM_LIMIT)


def _resident(shape):
    nd = len(shape)
    return pl.BlockSpec(shape, lambda *_: (0,) * nd, pipeline_mode=pl.Buffered(1))


def _sigmoid(x):
    return 1.0 / (1.0 + jnp.exp(-x))


def _log_sigmoid(x):
    return jnp.minimum(x, 0.0) - jnp.log1p(jnp.exp(-jnp.abs(x)))


def _softplus(x):
    return jnp.maximum(x, 0.0) + jnp.log1p(jnp.exp(-jnp.abs(x)))


def _rms(x, g):
    return x * lax.rsqrt(jnp.mean(x * x, axis=-1, keepdims=True) + EPS) * g


def _ada_kernel(c_ref, w_ref, b_ref, o_ref):
    c = c_ref[...]
    s = (c * _sigmoid(c)).astype(BF16)
    o_ref[...] = jnp.dot(s, w_ref[...].astype(BF16), preferred_element_type=F32) + b_ref[...]


def _ada(c, w, b):
    n, d = c.shape
    cols = w.shape[1]
    return pl.pallas_call(
        _ada_kernel,
        grid=(cols // d,),
        in_specs=[pl.BlockSpec((n, d), lambda j: (0, 0)),
                  pl.BlockSpec((d, d), lambda j: (0, j)),
                  pl.BlockSpec((1, d), lambda j: (0, j))],
        out_specs=pl.BlockSpec((n, d), lambda j: (0, j)),
        out_shape=jax.ShapeDtypeStruct((n, cols), F32),
        compiler_params=_params("arbitrary"),
        name="ada_modulation",
    )(c, w, b.reshape(1, cols))


def _mod_spec(rows_per_group, tm, r, chunk):
    tiles = rows_per_group // tm
    return pl.BlockSpec((1, r, D_MODEL), lambda i: (i // tiles, 0, chunk))


def _in_proj_kernel(x_ref, g_ref, sh_ref, sc_ref, w_ref, bf_ref,
                    xr_ref, gr_ref, q_ref, k_ref, v_ref, kb_ref, vb_ref, ga_ref, gb_ref, lf_ref):
    x = x_ref[...]
    h = (_rms(x, g_ref[...]) * (1.0 + sc_ref[0]) + sh_ref[0]).astype(BF16)

    def seg(s):
        return jnp.dot(h, w_ref[:, s * D_MODEL:(s + 1) * D_MODEL], preferred_element_type=F32)

    xr_ref[...] = seg(0)
    gr_ref[...] = seg(1)
    q_ref[...] = (seg(2) * (HEAD_DIM ** -0.5)).astype(BF16)
    k = seg(3)
    k_ref[...] = k
    kb_ref[...] = k.astype(BF16)
    v = seg(4)
    v_ref[...] = v
    vb_ref[...] = v.astype(BF16)
    ga_ref[...] = seg(5)
    gb_ref[...] = seg(6)
    fl = jnp.dot(h, w_ref[:, N_SEG * D_MODEL:], preferred_element_type=F32)
    lf_ref[...] = _log_sigmoid(fl + bf_ref[...])


def _in_proj(x, mod, rows_per_group, tm, norm_g, w_all, bf_pad):
    m = x.shape[0]
    r = mod.shape[1]
    row = lambda i: (i, 0)
    f32o = jax.ShapeDtypeStruct((m, D_MODEL), F32)
    bf16o = jax.ShapeDtypeStruct((m, D_MODEL), BF16)
    blk = pl.BlockSpec((tm, D_MODEL), row)
    return pl.pallas_call(
        _in_proj_kernel,
        grid=(m // tm,),
        in_specs=[blk, _resident((1, D_MODEL)),
                  _mod_spec(rows_per_group, tm, r, 0), _mod_spec(rows_per_group, tm, r, 1),
                  _resident((D_MODEL, W_ALL_COLS)), _resident((1, LANES))],
        out_specs=[blk] * 9 + [pl.BlockSpec((tm, LANES), row)],
        out_shape=[f32o, f32o, bf16o, f32o, f32o, bf16o, bf16o, f32o, f32o,
                   jax.ShapeDtypeStruct((m, LANES), F32)],
        compiler_params=_params("arbitrary"),
        name="in_proj",
    )(x, norm_g, mod, mod, w_all, bf_pad)


def _block_gates(xc, wr_ref, wi_ref, br, bi):
    xcb = xc.astype(BF16)
    n = D_MODEL // MXU_DIM
    pr = [jnp.dot(xcb[:, c * MXU_DIM:(c + 1) * MXU_DIM], wr_ref[c], preferred_element_type=F32)
          for c in range(n)]
    pi = [jnp.dot(xcb[:, c * MXU_DIM:(c + 1) * MXU_DIM], wi_ref[c], preferred_element_type=F32)
          for c in range(n)]
    gate_r = _sigmoid(jnp.concatenate(pr, axis=1) + br)
    gate_i = _sigmoid(jnp.concatenate(pi, axis=1) + bi)
    return gate_r, gate_i


def _decay_and_input(xc, gate_r, gate_i, lam, is_start):
    log_a = -RG_C * gate_r * _softplus(-lam)
    a = jnp.exp(log_a)
    mult = jnp.sqrt(jnp.tanh(-log_a) * (a * a + 1.0))
    if is_start is not None:
        mult = jnp.where(is_start, 1.0, mult)
    return a, mult * gate_i * xc


def _rglru_seq_kernel(xr_ref, gr_ref, cw_ref, cb_ref, wr_ref, wi_ref, br_ref, bi_ref, lam_ref,
                      o_ref, hl_ref, xbuf, a_s, u_s, hcar, *, tt):
    t = pl.program_id(1)
    seg_len = tt // SUBLANES

    @pl.when(t == 0)
    def _():
        xbuf[0:SUBLANES, :] = jnp.zeros((SUBLANES, D_MODEL), F32)
        hcar[...] = jnp.zeros_like(hcar)

    @pl.when(t != 0)
    def _():
        xbuf[0:SUBLANES, :] = xbuf[tt:tt + SUBLANES, :]

    xbuf[SUBLANES:, :] = xr_ref[...]
    xc = cb_ref[...] + cw_ref[CONV_W - 1:CONV_W, :] * xbuf[SUBLANES:, :]
    for j in range(CONV_W - 1):
        back = CONV_W - 1 - j
        xc = xc + cw_ref[j:j + 1, :] * xbuf[pl.ds(SUBLANES - back, tt), :]

    gate_r, gate_i = _block_gates(xc, wr_ref, wi_ref, br_ref[...], bi_ref[...])
    is_start = (lax.broadcasted_iota(jnp.int32, (tt, 1), 0) + t * tt) == 0
    a, u = _decay_and_input(xc, gate_r, gate_i, lam_ref[...], is_start)
    n_lane_chunks = D_MODEL // LANES
    for c in range(n_lane_chunks):
        a_s[c] = a[:, c * LANES:(c + 1) * LANES]
        u_s[c] = u[:, c * LANES:(c + 1) * LANES]

    for c in range(n_lane_chunks):
        hh = jnp.zeros((SUBLANES, LANES), F32)
        pp = jnp.ones((SUBLANES, LANES), F32)
        for r in range(seg_len):
            sl = pl.ds(r, SUBLANES, stride=seg_len)
            ar = a_s[c, sl, :]
            hh = ar * hh + u_s[c, sl, :]
            pp = ar * pp
            u_s[c, sl, :] = hh
            a_s[c, sl, :] = pp
        lanes = slice(c * LANES, (c + 1) * LANES)
        cy = hcar[0:1, lanes]
        carries = []
        for s in range(SUBLANES):
            carries.append(cy)
            cy = hh[s:s + 1, :] + pp[s:s + 1, :] * cy
        hcar[0:1, lanes] = cy
        hl_ref[0, :, lanes] = cy
        cin = jnp.concatenate(carries, axis=0)
        for r in range(seg_len):
            sl = pl.ds(r, SUBLANES, stride=seg_len)
            u_s[c, sl, :] = u_s[c, sl, :] + a_s[c, sl, :] * cin

    hr = jnp.concatenate([u_s[c] for c in range(n_lane_chunks)], axis=1)
    o_ref[...] = (hr * jax.nn.gelu(gr_ref[...])).astype(BF16)


def _rglru_seq(xr, gr, n_batch, seq, rg):
    tt = SCAN_TILE
    nt = seq // tt
    blk = pl.BlockSpec((tt, D_MODEL), lambda b, t: (b * nt + t, 0))
    nchunk = D_MODEL // MXU_DIM
    return pl.pallas_call(
        functools.partial(_rglru_seq_kernel, tt=tt),
        grid=(n_batch, nt),
        in_specs=[blk, blk, _resident((CONV_W, D_MODEL)), _resident((1, D_MODEL)),
                  _resident((nchunk, MXU_DIM, MXU_DIM)), _resident((nchunk, MXU_DIM, MXU_DIM)),
                  _resident((1, D_MODEL)), _resident((1, D_MODEL)), _resident((1, D_MODEL))],
        out_specs=[blk, pl.BlockSpec((1, 1, D_MODEL), lambda b, t: (b, 0, 0))],
        out_shape=[jax.ShapeDtypeStruct((n_batch * seq, D_MODEL), BF16),
                   jax.ShapeDtypeStruct((n_batch, 1, D_MODEL), F32)],
        scratch_shapes=[pltpu.VMEM((tt + SUBLANES, D_MODEL), F32),
                        pltpu.VMEM((D_MODEL // LANES, tt, LANES), F32),
                        pltpu.VMEM((D_MODEL // LANES, tt, LANES), F32),
                        pltpu.VMEM((SUBLANES, D_MODEL), F32)],
        compiler_params=_params("arbitrary", "arbitrary"),
        name="rglru_prompt",
    )(xr, gr, *rg)


def _rglru_step_kernel(xr_ref, gr_ref, st_ref, h0_ref, cw_ref, cb_ref, wr_ref, wi_ref, br_ref,
                       bi_ref, lam_ref, o_ref, h_ref, *, at_start):
    xr = xr_ref[...]
    xc = cb_ref[...] + cw_ref[CONV_W - 1:CONV_W, :] * xr
    for j in range(CONV_W - 1):
        xc = xc + cw_ref[j:j + 1, :] * st_ref[j]
    gate_r, gate_i = _block_gates(xc, wr_ref, wi_ref, br_ref[...], bi_ref[...])
    a, u = _decay_and_input(xc, gate_r, gate_i, lam_ref[...], True if at_start else None)
    h = a * h0_ref[...] + u
    h_ref[...] = h
    o_ref[...] = (h * jax.nn.gelu(gr_ref[...])).astype(BF16)


def _rglru_step(xr, gr, state_t, h0, rg, at_start):
    n = xr.shape[0]
    full = lambda shape: pl.BlockSpec(shape, lambda i: (0,) * len(shape))
    nchunk = D_MODEL // MXU_DIM
    return pl.pallas_call(
        functools.partial(_rglru_step_kernel, at_start=at_start),
        grid=(1,),
        in_specs=[full((n, D_MODEL)), full((n, D_MODEL)), full((CONV_W - 1, n, D_MODEL)),
                  full((n, D_MODEL)), full((CONV_W, D_MODEL)), full((1, D_MODEL)),
                  full((nchunk, MXU_DIM, MXU_DIM)), full((nchunk, MXU_DIM, MXU_DIM)),
                  full((1, D_MODEL)), full((1, D_MODEL)), full((1, D_MODEL))],
        out_specs=[full((n, D_MODEL)), full((n, D_MODEL))],
        out_shape=[jax.ShapeDtypeStruct((n, D_MODEL), BF16),
                   jax.ShapeDtypeStruct((n, D_MODEL), F32)],
        compiler_params=_params("arbitrary"),
        name="rglru_sample",
    )(xr, gr, state_t, h0, *rg)


def _cumsum_kernel(x_ref, o_ref, *, n):
    x = x_ref[0]
    lane = lax.broadcasted_iota(jnp.int32, x.shape, 1)
    d = 1
    while d < n:
        x = x + jnp.where(lane >= d, pltpu.roll(x, d, axis=1), 0.0)
        d *= 2
    o_ref[0] = x


def _cumsum_lanes(x):
    b, h, n = x.shape
    blk = pl.BlockSpec((1, h, n), lambda i: (i, 0, 0))
    return pl.pallas_call(
        functools.partial(_cumsum_kernel, n=n),
        grid=(b,),
        in_specs=[blk], out_specs=blk,
        out_shape=jax.ShapeDtypeStruct((b, h, n), F32),
        compiler_params=_params("arbitrary"),
        name="logf_cumsum",
    )(x)


def _fox_prompt_kernel(qt_ref, kt_ref, q_ref, k_ref, v_ref, fc_ref, fr_ref, o_ref,
                       m_s, l_s, acc_s, *, tq, tk):
    t = pl.program_id(2)
    qi = qt_ref[t]
    ki = kt_ref[t]

    @pl.when(ki == 0)
    def _():
        m_s[...] = jnp.full_like(m_s, -jnp.inf)
        l_s[...] = jnp.zeros_like(l_s)
        acc_s[...] = jnp.zeros_like(acc_s)

    q2 = q_ref[0]
    k2 = k_ref[0]
    v2 = v_ref[0]
    lane = lax.broadcasted_iota(jnp.int32, (1, LANES), 1)

    def update(masked):
        for h in range(2):
            in_head = (lane >= h * HEAD_DIM) & (lane < (h + 1) * HEAD_DIM)
            qh = jnp.where(in_head, q2, jnp.zeros_like(q2))
            s = lax.dot_general(qh, k2, (((1,), (1,)), ((), ())), preferred_element_type=F32)
            s = s + (fc_ref[0, h] - fr_ref[0, h])
            if masked:
                rowi = lax.broadcasted_iota(jnp.int32, (tq, tk), 0)
                coli = lax.broadcasted_iota(jnp.int32, (tq, tk), 1)
                s = jnp.where(coli <= rowi, s, -jnp.inf)
            m_prev = m_s[h]
            m_new = jnp.maximum(m_prev, jnp.max(s, axis=1, keepdims=True))
            alpha = jnp.exp(m_prev - m_new)
            p = jnp.exp(s - m_new)
            l_s[h] = alpha * l_s[h] + jnp.sum(p, axis=1, keepdims=True)
            acc_s[h] = alpha * acc_s[h] + jnp.dot(p.astype(BF16), v2, preferred_element_type=F32)
            m_s[h] = m_new

    @pl.when(ki < qi)
    def _():
        update(False)

    @pl.when(ki == qi)
    def _():
        update(True)
        o0 = acc_s[0] / l_s[0]
        o1 = acc_s[1] / l_s[1]
        o_ref[0] = jnp.where(lane < HEAD_DIM, o0, o1).astype(BF16)


def _fox_prompt(q, kb, vb, fcol, frow):
    b, s, _ = q.shape
    tq = tk = ATT_TILE
    nq = s // tq
    pairs = [(i, j) for i in range(nq) for j in range(i + 1)]
    qt = jnp.asarray(np.array([p[0] for p in pairs], np.int32))
    kt = jnp.asarray(np.array([p[1] for p in pairs], np.int32))
    hp = N_HEADS // 2
    grid_spec = pltpu.PrefetchScalarGridSpec(
        num_scalar_prefetch=2,
        grid=(b, hp, len(pairs)),
        in_specs=[
            pl.BlockSpec((1, tq, LANES), lambda bi, h, t, qt, kt: (bi, qt[t], h)),
            pl.BlockSpec((1, tk, LANES), lambda bi, h, t, qt, kt: (bi, kt[t], h)),
            pl.BlockSpec((1, tk, LANES), lambda bi, h, t, qt, kt: (bi, kt[t], h)),
            pl.BlockSpec((1, 2, tq, 1), lambda bi, h, t, qt, kt: (bi, h, qt[t], 0)),
            pl.BlockSpec((1, 2, 1, tk), lambda bi, h, t, qt, kt: (bi, h, 0, kt[t])),
        ],
        out_specs=pl.BlockSpec((1, tq, LANES), lambda bi, h, t, qt, kt: (bi, qt[t], h)),
        scratch_shapes=[pltpu.VMEM((2, tq, 1), F32), pltpu.VMEM((2, tq, 1), F32),
                        pltpu.VMEM((2, tq, LANES), F32)],
    )
    return pl.pallas_call(
        functools.partial(_fox_prompt_kernel, tq=tq, tk=tk),
        grid_spec=grid_spec,
        out_shape=jax.ShapeDtypeStruct((b, s, D_MODEL), BF16),
        compiler_params=_params("arbitrary", "arbitrary", "arbitrary"),
        name="fox_prompt",
    )(qt, kt, q, kb, vb, fcol, frow)


def _fox_paged_kernel(pt_ref, q_ref, kc_ref, vc_ref, lc_ref, *rest, n_steps, g):
    k_refs = rest[0:g]
    v_refs = rest[g:2 * g]
    lf_refs = rest[2 * g:3 * g]
    o_ref = rest[3 * g]
    qbd_s, m_s, l_s, acc_s, car_s = rest[3 * g + 1:]
    p = pl.program_id(1)
    head_of_lane = lax.broadcasted_iota(jnp.int32, (N_HEADS, D_MODEL), 1) // HEAD_DIM
    head = lax.broadcasted_iota(jnp.int32, (N_HEADS, D_MODEL), 0)
    own = head_of_lane == head

    @pl.when(p == 0)
    def _():
        qbd_s[...] = jnp.where(own, q_ref[0].astype(F32), 0.0).astype(BF16)
        m_s[...] = jnp.full_like(m_s, -jnp.inf)
        l_s[...] = jnp.zeros_like(l_s)
        acc_s[...] = jnp.zeros_like(acc_s)
        car_s[...] = jnp.zeros_like(car_s)

    qbd = qbd_s[...]
    r_i = lax.broadcasted_iota(jnp.int32, (PAGE_SIZE, PAGE_SIZE), 0)
    c_i = lax.broadcasted_iota(jnp.int32, (PAGE_SIZE, PAGE_SIZE), 1)
    tri = (r_i <= c_i).astype(F32)

    carry = car_s[...]
    logits = []
    vbs = []
    for j in range(g):
        kb = k_refs[j][0].astype(BF16)
        st = lax.dot_general(qbd, kb, (((1,), (1,)), ((), ())), preferred_element_type=F32)
        pref = lax.dot_general(lf_refs[j][0], tri, (((0,), (0,)), ((), ())),
                               precision=lax.Precision.HIGHEST, preferred_element_type=F32)
        logits.append(st - (carry + pref))
        carry = carry + pref[:, PAGE_SIZE - 1:PAGE_SIZE]
        vbs.append(v_refs[j][0].astype(BF16))
    car_s[...] = carry
    s = jnp.concatenate(logits, axis=1)
    m_prev = m_s[...]
    m_new = jnp.maximum(m_prev, jnp.max(s, axis=1, keepdims=True))
    alpha = jnp.exp(m_prev - m_new)
    pr = jnp.exp(s - m_new)
    l_s[...] = alpha * l_s[...] + jnp.sum(pr, axis=1, keepdims=True)
    prb = pr.astype(BF16)
    pv = jnp.dot(prb[:, 0:PAGE_SIZE], vbs[0], preferred_element_type=F32)
    for j in range(1, g):
        pv = pv + jnp.dot(prb[:, j * PAGE_SIZE:(j + 1) * PAGE_SIZE], vbs[j],
                          preferred_element_type=F32)
    acc_s[...] = alpha * acc_s[...] + pv
    m_s[...] = m_new

    @pl.when(p == n_steps - 1)
    def _():
        kc = kc_ref[0].astype(BF16).astype(F32)
        s_c = jnp.sum(qbd.astype(F32) * kc, axis=1, keepdims=True) - (carry + lc_ref[0])
        m_p = m_s[...]
        m_f = jnp.maximum(m_p, s_c)
        al = jnp.exp(m_p - m_f)
        p_c = jnp.exp(s_c - m_f)
        l_f = al * l_s[...] + p_c
        vc = vc_ref[0].astype(BF16).astype(F32)
        acc = al * acc_s[...] + p_c.astype(BF16).astype(F32) * vc
        o = jnp.where(own, acc / l_f, 0.0)
        o_ref[0] = jnp.sum(o, axis=0, keepdims=True).astype(BF16)


def _fox_paged(q, k_cur, v_cur, lf_cur, cache_k, cache_v, cache_lf, page_table):
    n, n_pages = page_table.shape
    g = PAGES_PER_STEP
    n_steps = n_pages // g
    row = pl.BlockSpec((1, 1, D_MODEL), lambda b, p, pt: (b, 0, 0))

    def page(width, j):
        return pl.BlockSpec((1, PAGE_SIZE, width), lambda b, p, pt: (pt[b, p * g + j], 0, 0))

    grid_spec = pltpu.PrefetchScalarGridSpec(
        num_scalar_prefetch=1,
        grid=(n, n_steps),
        in_specs=([row, row, row, pl.BlockSpec((1, N_HEADS, 1), lambda b, p, pt: (b, 0, 0))]
                  + [page(D_MODEL, j) for j in range(g)]
                  + [page(D_MODEL, j) for j in range(g)]
                  + [page(N_HEADS, j) for j in range(g)]),
        out_specs=row,
        scratch_shapes=[pltpu.VMEM((N_HEADS, D_MODEL), BF16), pltpu.VMEM((N_HEADS, 1), F32),
                        pltpu.VMEM((N_HEADS, 1), F32), pltpu.VMEM((N_HEADS, D_MODEL), F32),
                        pltpu.VMEM((N_HEADS, 1), F32)],
    )
    return pl.pallas_call(
        functools.partial(_fox_paged_kernel, n_steps=n_steps, g=g),
        grid_spec=grid_spec,
        out_shape=jax.ShapeDtypeStruct((n, 1, D_MODEL), BF16),
        compiler_params=_params("arbitrary", "arbitrary"),
        name="fox_paged",
    )(page_table, q, k_cur, v_cur, lf_cur, *([cache_k] * g), *([cache_v] * g), *([cache_lf] * g))


def _merge_kernel(x_ref, a_ref, b_ref, ga_ref, gb_ref, g1_ref, sh_ref, sc_ref, n2_ref,
                  wa_ref, wb_ref, wo_ref, x1_ref, h2_ref):
    ya = jnp.dot(a_ref[...], wa_ref[...], preferred_element_type=F32)
    yb = jnp.dot(b_ref[...], wb_ref[...], preferred_element_type=F32)
    mix = (_sigmoid(ga_ref[...]) * ya + _sigmoid(gb_ref[...]) * yb).astype(BF16)
    x1 = x_ref[...] + g1_ref[0] * jnp.dot(mix, wo_ref[...], preferred_element_type=F32)
    x1_ref[...] = x1
    h2_ref[...] = (_rms(x1, n2_ref[...]) * (1.0 + sc_ref[0]) + sh_ref[0]).astype(BF16)


def _merge(x, a_in, b_in, ga, gb, mod, rows_per_group, tm, norm_g, wa, wb, wo):
    m = x.shape[0]
    r = mod.shape[1]
    blk = pl.BlockSpec((tm, D_MODEL), lambda i: (i, 0))
    sq = _resident((D_MODEL, D_MODEL))
    return pl.pallas_call(
        _merge_kernel,
        grid=(m // tm,),
        in_specs=[blk, blk, blk, blk, blk,
                  _mod_spec(rows_per_group, tm, r, 2), _mod_spec(rows_per_group, tm, r, 3),
                  _mod_spec(rows_per_group, tm, r, 4), _resident((1, D_MODEL)), sq, sq, sq],
        out_specs=[blk, blk],
        out_shape=[jax.ShapeDtypeStruct((m, D_MODEL), F32), jax.ShapeDtypeStruct((m, D_MODEL), BF16)],
        compiler_params=_params("arbitrary"),
        name="merge_proj",
    )(x, a_in, b_in, ga, gb, mod, mod, mod, norm_g, wa, wb, wo)


def _ffn_kernel(x1_ref, h2_ref, g2_ref, nf_ref, wi_ref, wo_ref, y_ref):
    h2 = h2_ref[...]
    gu = jnp.dot(h2, wi_ref[:, 0:D_FF], preferred_element_type=F32)
    up = jnp.dot(h2, wi_ref[:, D_FF:2 * D_FF], preferred_element_type=F32)
    act = (gu * _sigmoid(gu) * up).astype(BF16)
    x2 = x1_ref[...] + g2_ref[0] * jnp.dot(act, wo_ref[...], preferred_element_type=F32)
    y_ref[...] = _rms(x2, nf_ref[...])


def _ffn(x1, h2, mod, rows_per_group, tm, normf_g, w_in, w_out):
    m = x1.shape[0]
    r = mod.shape[1]
    blk = pl.BlockSpec((tm, D_MODEL), lambda i: (i, 0))
    return pl.pallas_call(
        _ffn_kernel,
        grid=(m // tm,),
        in_specs=[blk, blk, _mod_spec(rows_per_group, tm, r, 5), _resident((1, D_MODEL)),
                  _resident((D_MODEL, 2 * D_FF)), _resident((D_FF, D_MODEL))],
        out_specs=blk,
        out_shape=jax.ShapeDtypeStruct((m, D_MODEL), F32),
        compiler_params=_params("arbitrary"),
        name="ffn_final",
    )(x1, h2, mod, normf_g, w_in, w_out)


def _pair_blocks(w):
    per = MXU_DIM // (D_MODEL // N_BLK)
    bw = D_MODEL // N_BLK
    w = w.reshape(N_BLK // per, per, bw, bw)
    eye = jnp.eye(per, dtype=w.dtype)
    return jnp.einsum("cpij,pq->cpiqj", w, eye).reshape(N_BLK // per, MXU_DIM, MXU_DIM).astype(BF16)


def kernel(x_prompt, x_sample, c_prompt, c_sample, cache_k, cache_v, cache_logf, state_conv,
           state_rglru, page_table, ada_w, ada_b, norm1_g, norm2_g, normf_g, w_in, b_f, conv_w,
           conv_b, rg_wr, rg_br, rg_wi, rg_bi, rg_lambda, w_proj_a, w_proj_b, w_o, w_ffn_in,
           w_ffn_out):
    depth = ada_w.shape[0]
    assert depth == 1, "single-layer trunk"
    bsz, seq, d = x_prompt.shape
    nsmp = x_sample.shape[0]
    assert x_sample.shape[1] == 1 and d == D_MODEL
    n_pool = cache_k.shape[1]
    past_len = page_table.shape[1] * PAGE_SIZE
    attn_w = N_HEADS * HEAD_DIM

    wl = w_in[0]
    cut = 2 * D_MODEL + 3 * attn_w
    w_all = jnp.concatenate(
        [wl[:, :cut], wl[:, cut + N_HEADS:], jnp.pad(wl[:, cut:cut + N_HEADS], ((0, 0), (0, LANES - N_HEADS)))],
        axis=1).astype(BF16)
    bf_pad = jnp.pad(b_f[0], (0, LANES - N_HEADS)).reshape(1, LANES)
    rg = (conv_w[0], conv_b[0].reshape(1, d), _pair_blocks(rg_wr[0]), _pair_blocks(rg_wi[0]),
          rg_br[0].reshape(1, d), rg_bi[0].reshape(1, d), rg_lambda[0].reshape(1, d))
    wa, wb, wo = (w[0].astype(BF16) for w in (w_proj_a, w_proj_b, w_o))
    wfi, wfo = w_ffn_in[0].astype(BF16), w_ffn_out[0].astype(BF16)
    n1, n2, nf = norm1_g[0].reshape(1, d), norm2_g[0].reshape(1, d), normf_g.reshape(1, d)

    mod = _ada(jnp.concatenate([c_prompt, c_sample], axis=0), ada_w[0], ada_b[0])
    mod_p = mod[:bsz].reshape(bsz, 1, 6 * d)
    mod_s = mod[bsz:].reshape(1, nsmp, 6 * d)

    xp = x_prompt.reshape(bsz * seq, d)
    xr, gr, q, k, v, kb, vb, ga, gb, lf = _in_proj(xp, mod_p, seq, ROW_TILE, n1, w_all, bf_pad)
    logf_p = lf[:, :N_HEADS].reshape(bsz, seq, N_HEADS)
    a_in, h_last_p = _rglru_seq(xr, gr, bsz, seq, rg)
    fcum = _cumsum_lanes(jnp.transpose(logf_p, (0, 2, 1)))
    att = _fox_prompt(q.reshape(bsz, seq, d), kb.reshape(bsz, seq, d), vb.reshape(bsz, seq, d),
                      fcum.reshape(bsz, N_HEADS, seq, 1), fcum.reshape(bsz, N_HEADS, 1, seq))
    x1, h2 = _merge(xp, a_in, att.reshape(bsz * seq, d), ga, gb, mod_p, seq, ROW_TILE, n2, wa, wb, wo)
    y_p = _ffn(x1, h2, mod_p, seq, ROW_TILE, nf, wfi, wfo).reshape(bsz, seq, d)
    conv_p = xr.reshape(bsz, seq, d)[:, seq - (CONV_W - 1):]

    xs = x_sample.reshape(nsmp, d)
    xr_s, gr_s, q_s, k_s, v_s, _, _, ga_s, gb_s, lf_s = _in_proj(xs, mod_s, nsmp, nsmp, n1, w_all, bf_pad)
    logf_s = lf_s[:, :N_HEADS]
    a_in_s, h_s = _rglru_step(xr_s, gr_s, jnp.transpose(state_conv[0], (1, 0, 2)), state_rglru[0],
                              rg, past_len == 0)
    att_s = _fox_paged(q_s.reshape(nsmp, 1, d), k_s.reshape(nsmp, 1, d), v_s.reshape(nsmp, 1, d),
                       logf_s.reshape(nsmp, N_HEADS, 1),
                       cache_k[0].reshape(n_pool, PAGE_SIZE, attn_w),
                       cache_v[0].reshape(n_pool, PAGE_SIZE, attn_w),
                       cache_logf[0], page_table)
    x1_s, h2_s = _merge(xs, a_in_s, att_s.reshape(nsmp, d), ga_s, gb_s, mod_s, nsmp, nsmp, n2, wa, wb, wo)
    y_s = _ffn(x1_s, h2_s, mod_s, nsmp, nsmp, nf, wfi, wfo).reshape(nsmp, 1, d)
    conv_s = jnp.concatenate([state_conv[0][:, 1:], xr_s[:, None, :]], axis=1)

    hd = (N_HEADS, HEAD_DIM)
    return (y_p, y_s,
            k.reshape(1, bsz, seq, *hd), v.reshape(1, bsz, seq, *hd), logf_p[None],
            conv_p[None], h_last_p.reshape(1, bsz, d),
            k_s.reshape(1, nsmp, 1, *hd), v_s.reshape(1, nsmp, 1, *hd),
            logf_s.reshape(1, nsmp, 1, N_HEADS),
            conv_s[None], h_s[None])
```

```python
import functools

import numpy as np
import jax
import jax.numpy as jnp
from jax import lax
from jax.experimental import pallas as pl
from jax.experimental.pallas import tpu as pltpu

F32 = jnp.float32
BF16 = jnp.bfloat16

D_MODEL = 1024
N_HEADS = 16
HEAD_DIM = 64
N_BLK = 16
CONV_W = 4
RG_C = 8.0
PAGE_SIZE = 128
D_FF = 2816
EPS = 1e-6

LANES = 128
SUBLANES = 8
MXU_DIM = 256
VMEM_LIMIT = 56 * 1024 * 1024

ROW_TILE = 256
SCAN_TILE = 256
ATT_TILE = 512
ATT_HEADS = 4
PAGES_PER_STEP = 4

N_SEG = 7
W_ALL_COLS = N_SEG * D_MODEL + LANES

AUG = HEAD_DIM
N_SPLIT = 3
ONES_LANE = N_SPLIT * N_HEADS


def _params(*sem):
    return pltpu.CompilerParams(dimension_semantics=sem, vmem_limit_bytes=VMEM_LIMIT)


def _resident(shape):
    nd = len(shape)
    return pl.BlockSpec(shape, lambda *_: (0,) * nd, pipeline_mode=pl.Buffered(1))


def _sigmoid(x):
    return 1.0 / (1.0 + jnp.exp(-x))


def _log_sigmoid(x):
    return jnp.minimum(x, 0.0) - jnp.log1p(jnp.exp(-jnp.abs(x)))


def _softplus(x):
    return jnp.maximum(x, 0.0) + jnp.log1p(jnp.exp(-jnp.abs(x)))


def _rms(x, g):
    return x * lax.rsqrt(jnp.mean(x * x, axis=-1, keepdims=True) + EPS) * g


def _ada_kernel(c_ref, w_ref, b_ref, o_ref):
    c = c_ref[...]
    s = (c * _sigmoid(c)).astype(BF16)
    o_ref[...] = jnp.dot(s, w_ref[...].astype(BF16), preferred_element_type=F32) + b_ref[...]


def _ada(c, w, b):
    n, d = c.shape
    cols = w.shape[1]
    return pl.pallas_call(
        _ada_kernel,
        grid=(cols // d,),
        in_specs=[pl.BlockSpec((n, d), lambda j: (0, 0)),
                  pl.BlockSpec((d, d), lambda j: (0, j)),
                  pl.BlockSpec((1, d), lambda j: (0, j))],
        out_specs=pl.BlockSpec((n, d), lambda j: (0, j)),
        out_shape=jax.ShapeDtypeStruct((n, cols), F32),
        compiler_params=_params("arbitrary"),
        name="ada_modulation",
    )(c, w, b.reshape(1, cols))


def _mod_spec(rows_per_group, tm, r, chunk):
    tiles = rows_per_group // tm
    return pl.BlockSpec((1, r, D_MODEL), lambda i: (i // tiles, 0, chunk))


def _split3(x):
    hi = x.astype(BF16).astype(F32)
    r = x - hi
    mid = r.astype(BF16).astype(F32)
    return hi, mid, r - mid


def _in_proj_kernel(*refs, prompt, tiles_per_seq):
    if prompt:
        (x_ref, g_ref, sh_ref, sc_ref, w_ref, bf_ref, tri_ref, pq_ref, pk_ref,
         xr_ref, gr_ref, k_ref, v_ref, ga_ref, gb_ref, lf_ref, qa_ref, ka_ref, vt_ref, fcar) = refs
    else:
        (x_ref, g_ref, sh_ref, sc_ref, w_ref, bf_ref,
         xr_ref, gr_ref, k_ref, v_ref, ga_ref, gb_ref, lf_ref, q_ref) = refs
    x = x_ref[...]
    h = (_rms(x, g_ref[...]) * (1.0 + sc_ref[0]) + sh_ref[0]).astype(BF16)

    def seg(s):
        return jnp.dot(h, w_ref[:, s * D_MODEL:(s + 1) * D_MODEL], preferred_element_type=F32)

    xr_ref[...] = seg(0)
    gr_ref[...] = seg(1)
    q = (seg(2) * (HEAD_DIM ** -0.5)).astype(BF16)
    k = seg(3)
    k_ref[...] = k
    v = seg(4)
    v_ref[...] = v
    ga_ref[...] = seg(5)
    gb_ref[...] = seg(6)
    fl = jnp.dot(h, w_ref[:, N_SEG * D_MODEL:], preferred_element_type=F32)
    lf = _log_sigmoid(fl + bf_ref[...])
    lf_ref[...] = lf
    if not prompt:
        q_ref[...] = q
        return

    tm = x.shape[0]
    vt_ref[...] = v.T.astype(BF16)

    @pl.when(pl.program_id(0) % tiles_per_seq == 0)
    def _():
        fcar[...] = jnp.zeros_like(fcar)

    lane = lax.broadcasted_iota(jnp.int32, (1, LANES), 1)
    tri = tri_ref[...]
    cum = fcar[...]
    for piece in _split3(jnp.where(lane < N_HEADS, lf, 0.0)):
        cum = cum + jnp.dot(tri, piece.astype(BF16), preferred_element_type=F32)
    fcar[...] = cum[tm - 1:tm, :]

    c_hi, c_mid, c_lo = _split3(cum)
    faug = jnp.where(lane < N_HEADS, c_hi,
                     jnp.where(lane < 2 * N_HEADS, pltpu.roll(c_mid, N_HEADS, axis=1),
                               jnp.where(lane < ONES_LANE, pltpu.roll(c_lo, 2 * N_HEADS, axis=1),
                                         jnp.where(lane == ONES_LANE, 1.0, 0.0)))).astype(BF16)
    kb = k.astype(BF16)
    for j in range(N_HEADS // 2):
        pair = slice(j * LANES, (j + 1) * LANES)
        out = slice(j * MXU_DIM, (j + 1) * MXU_DIM)
        qa_ref[:, out] = jnp.dot(jnp.concatenate([q[:, pair], faug], axis=1), pq_ref[j],
                                 preferred_element_type=F32).astype(BF16)
        ka_ref[:, out] = jnp.dot(jnp.concatenate([kb[:, pair], faug], axis=1), pk_ref[j],
                                 preferred_element_type=F32).astype(BF16)


def _placement_matrices():
    n_pairs = N_HEADS // 2
    pq = np.zeros((n_pairs, MXU_DIM, MXU_DIM), np.float32)
    pk = np.zeros((n_pairs, MXU_DIM, MXU_DIM), np.float32)
    for j in range(n_pairs):
        for s in range(2):
            head = 2 * j + s
            for d in range(HEAD_DIM):
                pq[j, s * HEAD_DIM + d, s * LANES + d] = 1.0
                pk[j, s * HEAD_DIM + d, s * LANES + d] = 1.0
            cb = s * LANES + AUG
            for i in range(N_SPLIT):
                pq[j, LANES + i * N_HEADS + head, cb + i] = 1.0
                pq[j, LANES + ONES_LANE, cb + N_SPLIT + i] = 1.0
                pk[j, LANES + ONES_LANE, cb + i] = 1.0
                pk[j, LANES + i * N_HEADS + head, cb + N_SPLIT + i] = -1.0
    return jnp.asarray(pq, BF16), jnp.asarray(pk, BF16)


def _in_proj(x, mod, rows_per_group, tm, norm_g, w_all, bf_pad, prompt):
    m = x.shape[0]
    r = mod.shape[1]
    row = lambda i: (i, 0)
    f32o = jax.ShapeDtypeStruct((m, D_MODEL), F32)
    blk = pl.BlockSpec((tm, D_MODEL), row)
    in_specs = [blk, _resident((1, D_MODEL)),
                _mod_spec(rows_per_group, tm, r, 0), _mod_spec(rows_per_group, tm, r, 1),
                _resident((D_MODEL, W_ALL_COLS)), _resident((1, LANES))]
    operands = [x, norm_g, mod, mod, w_all, bf_pad]
    out_specs = [blk] * 6 + [pl.BlockSpec((tm, LANES), row)]
    out_shape = [f32o] * 6 + [jax.ShapeDtypeStruct((m, LANES), F32)]
    scratch = []
    if prompt:
        tri = jnp.asarray(np.tril(np.ones((tm, tm), np.float32)), BF16)
        pq, pk = _placement_matrices()
        in_specs += [_resident(tri.shape), _resident(pq.shape), _resident(pk.shape)]
        operands += [tri, pq, pk]
        wide = pl.BlockSpec((tm, 2 * D_MODEL), row)
        out_specs += [wide, wide, pl.BlockSpec((D_MODEL, tm), lambda i: (0, i))]
        out_shape += [jax.ShapeDtypeStruct((m, 2 * D_MODEL), BF16)] * 2
        out_shape += [jax.ShapeDtypeStruct((D_MODEL, m), BF16)]
        scratch = [pltpu.VMEM((1, LANES), F32)]
    else:
        out_specs += [blk]
        out_shape += [jax.ShapeDtypeStruct((m, D_MODEL), BF16)]
    return pl.pallas_call(
        functools.partial(_in_proj_kernel, prompt=prompt, tiles_per_seq=rows_per_group // tm),
        grid=(m // tm,),
        in_specs=in_specs, out_specs=out_specs, out_shape=out_shape, scratch_shapes=scratch,
        compiler_params=_params("arbitrary"),
        name="in_proj_prompt" if prompt else "in_proj_sample",
    )(*operands)


def _block_gates(xc, wr_ref, wi_ref, br, bi):
    xcb = xc.astype(BF16)
    n = D_MODEL // MXU_DIM
    pr = [jnp.dot(xcb[:, c * MXU_DIM:(c + 1) * MXU_DIM], wr_ref[c], preferred_element_type=F32)
          for c in range(n)]
    pi = [jnp.dot(xcb[:, c * MXU_DIM:(c + 1) * MXU_DIM], wi_ref[c], preferred_element_type=F32)
          for c in range(n)]
    gate_r = _sigmoid(jnp.concatenate(pr, axis=1) + br)
    gate_i = _sigmoid(jnp.concatenate(pi, axis=1) + bi)
    return gate_r, gate_i


def _decay_and_input(xc, gate_r, gate_i, lam, is_start):
    log_a = -RG_C * gate_r * _softplus(-lam)
    a = jnp.exp(log_a)
    mult = jnp.sqrt(jnp.tanh(-log_a) * (a * a + 1.0))
    if is_start is not None:
        mult = jnp.where(is_start, 1.0, mult)
    return a, mult * gate_i * xc


def _rglru_seq_kernel(xr_ref, gr_ref, cw_ref, cb_ref, wr_ref, wi_ref, br_ref, bi_ref, lam_ref,
                      o_ref, hl_ref, xbuf, a_s, u_s, hcar, *, tt):
    t = pl.program_id(1)
    seg_len = tt // SUBLANES

    @pl.when(t == 0)
    def _():
        xbuf[0:SUBLANES, :] = jnp.zeros((SUBLANES, D_MODEL), F32)
        hcar[...] = jnp.zeros_like(hcar)

    @pl.when(t != 0)
    def _():
        xbuf[0:SUBLANES, :] = xbuf[tt:tt + SUBLANES, :]

    xbuf[SUBLANES:, :] = xr_ref[...]
    xc = cb_ref[...] + cw_ref[CONV_W - 1:CONV_W, :] * xbuf[SUBLANES:, :]
    for j in range(CONV_W - 1):
        back = CONV_W - 1 - j
        xc = xc + cw_ref[j:j + 1, :] * xbuf[pl.ds(SUBLANES - back, tt), :]

    gate_r, gate_i = _block_gates(xc, wr_ref, wi_ref, br_ref[...], bi_ref[...])
    is_start = (lax.broadcasted_iota(jnp.int32, (tt, 1), 0) + t * tt) == 0
    a, u = _decay_and_input(xc, gate_r, gate_i, lam_ref[...], is_start)
    n_lane_chunks = D_MODEL // LANES
    for c in range(n_lane_chunks):
        a_s[c] = a[:, c * LANES:(c + 1) * LANES]
        u_s[c] = u[:, c * LANES:(c + 1) * LANES]

    for c in range(n_lane_chunks):
        hh = jnp.zeros((SUBLANES, LANES), F32)
        pp = jnp.ones((SUBLANES, LANES), F32)
        for r in range(seg_len):
            sl = pl.ds(r, SUBLANES, stride=seg_len)
            ar = a_s[c, sl, :]
            hh = ar * hh + u_s[c, sl, :]
            pp = ar * pp
            u_s[c, sl, :] = hh
            a_s[c, sl, :] = pp
        lanes = slice(c * LANES, (c + 1) * LANES)
        cy = hcar[0:1, lanes]
        carries = []
        for s in range(SUBLANES):
            carries.append(cy)
            cy = hh[s:s + 1, :] + pp[s:s + 1, :] * cy
        hcar[0:1, lanes] = cy
        hl_ref[0, :, lanes] = cy
        cin = jnp.concatenate(carries, axis=0)
        for r in range(seg_len):
            sl = pl.ds(r, SUBLANES, stride=seg_len)
            u_s[c, sl, :] = u_s[c, sl, :] + a_s[c, sl, :] * cin

    hr = jnp.concatenate([u_s[c] for c in range(n_lane_chunks)], axis=1)
    o_ref[...] = (hr * jax.nn.gelu(gr_ref[...])).astype(BF16)


def _rglru_seq(xr, gr, n_batch, seq, rg):
    tt = SCAN_TILE
    nt = seq // tt
    blk = pl.BlockSpec((tt, D_MODEL), lambda b, t: (b * nt + t, 0))
    nchunk = D_MODEL // MXU_DIM
    return pl.pallas_call(
        functools.partial(_rglru_seq_kernel, tt=tt),
        grid=(n_batch, nt),
        in_specs=[blk, blk, _resident((CONV_W, D_MODEL)), _resident((1, D_MODEL)),
                  _resident((nchunk, MXU_DIM, MXU_DIM)), _resident((nchunk, MXU_DIM, MXU_DIM)),
                  _resident((1, D_MODEL)), _resident((1, D_MODEL)), _resident((1, D_MODEL))],
        out_specs=[blk, pl.BlockSpec((1, 1, D_MODEL), lambda b, t: (b, 0, 0))],
        out_shape=[jax.ShapeDtypeStruct((n_batch * seq, D_MODEL), BF16),
                   jax.ShapeDtypeStruct((n_batch, 1, D_MODEL), F32)],
        scratch_shapes=[pltpu.VMEM((tt + SUBLANES, D_MODEL), F32),
                        pltpu.VMEM((D_MODEL // LANES, tt, LANES), F32),
                        pltpu.VMEM((D_MODEL // LANES, tt, LANES), F32),
                        pltpu.VMEM((SUBLANES, D_MODEL), F32)],
        compiler_params=_params("arbitrary", "arbitrary"),
        name="rglru_prompt",
    )(xr, gr, *rg)


def _rglru_step_kernel(xr_ref, gr_ref, st_ref, h0_ref, cw_ref, cb_ref, wr_ref, wi_ref, br_ref,
                       bi_ref, lam_ref, o_ref, h_ref, *, at_start):
    xr = xr_ref[...]
    xc = cb_ref[...] + cw_ref[CONV_W - 1:CONV_W, :] * xr
    for j in range(CONV_W - 1):
        xc = xc + cw_ref[j:j + 1, :] * st_ref[j]
    gate_r, gate_i = _block_gates(xc, wr_ref, wi_ref, br_ref[...], bi_ref[...])
    a, u = _decay_and_input(xc, gate_r, gate_i, lam_ref[...], True if at_start else None)
    h = a * h0_ref[...] + u
    h_ref[...] = h
    o_ref[...] = (h * jax.nn.gelu(gr_ref[...])).astype(BF16)


def _rglru_step(xr, gr, state_t, h0, rg, at_start):
    n = xr.shape[0]
    full = lambda shape: pl.BlockSpec(shape, lambda i: (0,) * len(shape))
    nchunk = D_MODEL // MXU_DIM
    return pl.pallas_call(
        functools.partial(_rglru_step_kernel, at_start=at_start),
        grid=(1,),
        in_specs=[full((n, D_MODEL)), full((n, D_MODEL)), full((CONV_W - 1, n, D_MODEL)),
                  full((n, D_MODEL)), full((CONV_W, D_MODEL)), full((1, D_MODEL)),
                  full((nchunk, MXU_DIM, MXU_DIM)), full((nchunk, MXU_DIM, MXU_DIM)),
                  full((1, D_MODEL)), full((1, D_MODEL)), full((1, D_MODEL))],
        out_specs=[full((n, D_MODEL)), full((n, D_MODEL))],
        out_shape=[jax.ShapeDtypeStruct((n, D_MODEL), BF16),
                   jax.ShapeDtypeStruct((n, D_MODEL), F32)],
        compiler_params=_params("arbitrary"),
        name="rglru_sample",
    )(xr, gr, state_t, h0, *rg)


def _fox_prompt_kernel(qt_ref, kt_ref, q_ref, k_ref, vt_ref, o_ref, m_s, l_s, acc_s, *, tq, tk):
    t = pl.program_id(2)
    qi = qt_ref[t]
    ki = kt_ref[t]

    @pl.when(ki == 0)
    def _():
        m_s[...] = jnp.full_like(m_s, -jnp.inf)
        l_s[...] = jnp.zeros_like(l_s)
        acc_s[...] = jnp.zeros_like(acc_s)

    def update(masked):
        for h in range(ATT_HEADS):
            lanes = slice(h * LANES, (h + 1) * LANES)
            st = lax.dot_general(k_ref[:, lanes], q_ref[:, lanes], (((1,), (1,)), ((), ())),
                                 preferred_element_type=F32)
            if masked:
                kpos = lax.broadcasted_iota(jnp.int32, (tk, tq), 0)
                qpos = lax.broadcasted_iota(jnp.int32, (tk, tq), 1)
                st = jnp.where(kpos <= qpos, st, -jnp.inf)
            m_prev = m_s[h]
            m_new = jnp.maximum(m_prev, jnp.max(st, axis=0, keepdims=True))
            alpha = jnp.exp(m_prev - m_new)
            p = jnp.exp(st - m_new)
            l_s[h] = alpha * l_s[h] + jnp.sum(p, axis=0, keepdims=True)
            pv = jnp.dot(vt_ref[h * HEAD_DIM:(h + 1) * HEAD_DIM, :], p.astype(BF16),
                         preferred_element_type=F32)
            acc_s[h] = alpha * acc_s[h] + pv
            m_s[h] = m_new

    @pl.when(ki < qi)
    def _():
        update(False)

    @pl.when(ki == qi)
    def _():
        update(True)
        ot = jnp.concatenate([acc_s[h] / l_s[h] for h in range(ATT_HEADS)], axis=0)
        o_ref[...] = ot.T.astype(BF16)


def _fox_prompt(q_aug, k_aug, vt, n_batch, seq):
    tq = tk = ATT_TILE
    nq = seq // tq
    pairs = [(i, j) for i in range(nq) for j in range(i + 1)]
    qt = jnp.asarray(np.array([p[0] for p in pairs], np.int32))
    kt = jnp.asarray(np.array([p[1] for p in pairs], np.int32))
    groups = N_HEADS // ATT_HEADS
    wide = ATT_HEADS * LANES
    narrow = ATT_HEADS * HEAD_DIM
    grid_spec = pltpu.PrefetchScalarGridSpec(
        num_scalar_prefetch=2,
        grid=(n_batch, groups, len(pairs)),
        in_specs=[
            pl.BlockSpec((tq, wide), lambda b, g, t, qt, kt: (b * nq + qt[t], g)),
            pl.BlockSpec((tk, wide), lambda b, g, t, qt, kt: (b * nq + kt[t], g)),
            pl.BlockSpec((narrow, tk), lambda b, g, t, qt, kt: (g, b * nq + kt[t])),
        ],
        out_specs=pl.BlockSpec((tq, narrow), lambda b, g, t, qt, kt: (b * nq + qt[t], g)),
        scratch_shapes=[pltpu.VMEM((ATT_HEADS, 1, tq), F32), pltpu.VMEM((ATT_HEADS, 1, tq), F32),
                        pltpu.VMEM((ATT_HEADS, HEAD_DIM, tq), F32)],
    )
    return pl.pallas_call(
        functools.partial(_fox_prompt_kernel, tq=tq, tk=tk),
        grid_spec=grid_spec,
        out_shape=jax.ShapeDtypeStruct((n_batch * seq, D_MODEL), BF16),
        compiler_params=_params("arbitrary", "arbitrary", "arbitrary"),
        name="fox_prompt",
    )(qt, kt, q_aug, k_aug, vt)


PAGE_COLS = PAGE_SIZE * N_HEADS


def _fox_paged_kernel(pt_ref, q_ref, kc_ref, vc_ref, lc_ref, tri_ref, *rest, n_steps, g):
    k_refs = rest[0:g]
    v_refs = rest[g:2 * g]
    lf_refs = rest[2 * g:3 * g]
    o_ref = rest[3 * g]
    m_s, l_s, acc_s, car_s = rest[3 * g + 1:]
    p = pl.program_id(1)

    @pl.when(p == 0)
    def _():
        m_s[...] = jnp.full_like(m_s, -jnp.inf)
        l_s[...] = jnp.zeros_like(l_s)
        acc_s[...] = jnp.zeros_like(acc_s)
        car_s[...] = jnp.zeros_like(car_s)

    q16 = q_ref[0]
    col_head = lax.broadcasted_iota(jnp.int32, (N_HEADS, PAGE_COLS), 1) % N_HEADS
    own = col_head == lax.broadcasted_iota(jnp.int32, (N_HEADS, PAGE_COLS), 0)
    tri = tri_ref[...]

    carry = car_s[...]
    logits = []
    v2s = []
    for j in range(g):
        k2 = k_refs[j][0, 0].reshape(PAGE_COLS, HEAD_DIM).astype(BF16)
        st = lax.dot_general(q16, k2, (((1,), (1,)), ((), ())), preferred_element_type=F32)
        pref = jnp.zeros((N_HEADS, PAGE_COLS), F32)
        for piece in _split3(lf_refs[j][0, 0].T):
            pref = pref + jnp.dot(piece.astype(BF16), tri, preferred_element_type=F32)
        logits.append(jnp.where(own, st - (carry + pref), -jnp.inf))
        carry = carry + pref[:, PAGE_COLS - 1:PAGE_COLS]
        v2s.append(v_refs[j][0, 0].reshape(PAGE_COLS, HEAD_DIM).astype(BF16))
    car_s[...] = carry
    s = jnp.concatenate(logits, axis=1)
    m_prev = m_s[...]
    m_new = jnp.maximum(m_prev, jnp.max(s, axis=1, keepdims=True))
    alpha = jnp.exp(m_prev - m_new)
    pr = jnp.exp(s - m_new)
    l_new = alpha * l_s[...] + jnp.sum(pr, axis=1, keepdims=True)
    prb = pr.astype(BF16)
    pv = jnp.dot(prb[:, 0:PAGE_COLS], v2s[0], preferred_element_type=F32)
    for j in range(1, g):
        pv = pv + jnp.dot(prb[:, j * PAGE_COLS:(j + 1) * PAGE_COLS], v2s[j],
                          preferred_element_type=F32)
    acc_new = alpha * acc_s[...] + pv
    l_s[...] = l_new
    acc_s[...] = acc_new
    m_s[...] = m_new

    @pl.when(p == n_steps - 1)
    def _():
        kc = kc_ref[0].astype(BF16).astype(F32)
        s_c = jnp.sum(q16.astype(F32) * kc, axis=1, keepdims=True) - (carry + lc_ref[0])
        m_f = jnp.maximum(m_new, s_c)
        al = jnp.exp(m_new - m_f)
        p_c = jnp.exp(s_c - m_f)
        l_f = al * l_new + p_c
        vc = vc_ref[0].astype(BF16).astype(F32)
        acc = al * acc_new + p_c.astype(BF16).astype(F32) * vc
        o_ref[0] = (acc / l_f).astype(BF16)


def _fox_paged(q16, k_cur, v_cur, lf_cur, cache_k, cache_v, cache_lf, page_table):
    n, n_pages = page_table.shape
    g = PAGES_PER_STEP
    n_steps = n_pages // g
    per_seq = lambda shape: pl.BlockSpec((1,) + shape, lambda b, p, pt: (b, 0, 0))

    def page(tail, j):
        zeros = (0,) * len(tail)
        return pl.BlockSpec((1, 1) + tail, lambda b, p, pt: (0, pt[b, p * g + j]) + zeros)

    tri = jnp.asarray(np.repeat(np.triu(np.ones((PAGE_SIZE, PAGE_SIZE), np.float32)), N_HEADS, axis=1),
                      BF16)
    grid_spec = pltpu.PrefetchScalarGridSpec(
        num_scalar_prefetch=1,
        grid=(n, n_steps),
        in_specs=([per_seq((N_HEADS, HEAD_DIM))] * 3 + [per_seq((N_HEADS, 1))]
                  + [pl.BlockSpec(tri.shape, lambda b, p, pt: (0, 0), pipeline_mode=pl.Buffered(1))]
                  + [page((PAGE_SIZE, N_HEADS, HEAD_DIM), j) for j in range(g)]
                  + [page((PAGE_SIZE, N_HEADS, HEAD_DIM), j) for j in range(g)]
                  + [page((PAGE_SIZE, N_HEADS), j) for j in range(g)]),
        out_specs=per_seq((N_HEADS, HEAD_DIM)),
        scratch_shapes=[pltpu.VMEM((N_HEADS, 1), F32), pltpu.VMEM((N_HEADS, 1), F32),
                        pltpu.VMEM((N_HEADS, HEAD_DIM), F32), pltpu.VMEM((N_HEADS, 1), F32)],
    )
    return pl.pallas_call(
        functools.partial(_fox_paged_kernel, n_steps=n_steps, g=g),
        grid_spec=grid_spec,
        out_shape=jax.ShapeDtypeStruct((n, N_HEADS, HEAD_DIM), BF16),
        compiler_params=_params("arbitrary", "arbitrary"),
        name="fox_paged",
    )(page_table, q16, k_cur, v_cur, lf_cur, tri,
      *([cache_k] * g), *([cache_v] * g), *([cache_lf] * g))


def _merge_kernel(x_ref, a_ref, b_ref, ga_ref, gb_ref, g1_ref, sh_ref, sc_ref, n2_ref,
                  wa_ref, wb_ref, wo_ref, x1_ref, h2_ref):
    ya = jnp.dot(a_ref[...], wa_ref[...], preferred_element_type=F32)
    yb = jnp.dot(b_ref[...], wb_ref[...], preferred_element_type=F32)
    mix = (_sigmoid(ga_ref[...]) * ya + _sigmoid(gb_ref[...]) * yb).astype(BF16)
    x1 = x_ref[...] + g1_ref[0] * jnp.dot(mix, wo_ref[...], preferred_element_type=F32)
    x1_ref[...] = x1
    h2_ref[...] = (_rms(x1, n2_ref[...]) * (1.0 + sc_ref[0]) + sh_ref[0]).astype(BF16)


def _merge(x, a_in, b_in, ga, gb, mod, rows_per_group, tm, norm_g, wa, wb, wo):
    m = x.shape[0]
    r = mod.shape[1]
    blk = pl.BlockSpec((tm, D_MODEL), lambda i: (i, 0))
    sq = _resident((D_MODEL, D_MODEL))
    return pl.pallas_call(
        _merge_kernel,
        grid=(m // tm,),
        in_specs=[blk, blk, blk, blk, blk,
                  _mod_spec(rows_per_group, tm, r, 2), _mod_spec(rows_per_group, tm, r, 3),
                  _mod_spec(rows_per_group, tm, r, 4), _resident((1, D_MODEL)), sq, sq, sq],
        out_specs=[blk, blk],
        out_shape=[jax.ShapeDtypeStruct((m, D_MODEL), F32), jax.ShapeDtypeStruct((m, D_MODEL), BF16)],
        compiler_params=_params("arbitrary"),
        name="merge_proj",
    )(x, a_in, b_in, ga, gb, mod, mod, mod, norm_g, wa, wb, wo)


def _ffn_kernel(x1_ref, h2_ref, g2_ref, nf_ref, wi_ref, wo_ref, y_ref):
    h2 = h2_ref[...]
    gu = jnp.dot(h2, wi_ref[:, 0:D_FF], preferred_element_type=F32)
    up = jnp.dot(h2, wi_ref[:, D_FF:2 * D_FF], preferred_element_type=F32)
    act = (gu * _sigmoid(gu) * up).astype(BF16)
    x2 = x1_ref[...] + g2_ref[0] * jnp.dot(act, wo_ref[...], preferred_element_type=F32)
    y_ref[...] = _rms(x2, nf_ref[...])


def _ffn(x1, h2, mod, rows_per_group, tm, normf_g, w_in, w_out):
    m = x1.shape[0]
    r = mod.shape[1]
    blk = pl.BlockSpec((tm, D_MODEL), lambda i: (i, 0))
    return pl.pallas_call(
        _ffn_kernel,
        grid=(m // tm,),
        in_specs=[blk, blk, _mod_spec(rows_per_group, tm, r, 5), _resident((1, D_MODEL)),
                  _resident((D_MODEL, 2 * D_FF)), _resident((D_FF, D_MODEL))],
        out_specs=blk,
        out_shape=jax.ShapeDtypeStruct((m, D_MODEL), F32),
        compiler_params=_params("arbitrary"),
        name="ffn_final",
    )(x1, h2, mod, normf_g, w_in, w_out)


def _pair_blocks(w):
    per = MXU_DIM // (D_MODEL // N_BLK)
    bw = D_MODEL // N_BLK
    w = w.reshape(N_BLK // per, per, bw, bw)
    eye = jnp.eye(per, dtype=w.dtype)
    return jnp.einsum("cpij,pq->cpiqj", w, eye).reshape(N_BLK // per, MXU_DIM, MXU_DIM).astype(BF16)


def kernel(x_prompt, x_sample, c_prompt, c_sample, cache_k, cache_v, cache_logf, state_conv,
           state_rglru, page_table, ada_w, ada_b, norm1_g, norm2_g, normf_g, w_in, b_f, conv_w,
           conv_b, rg_wr, rg_br, rg_wi, rg_bi, rg_lambda, w_proj_a, w_proj_b, w_o, w_ffn_in,
           w_ffn_out):
    depth = ada_w.shape[0]
    assert depth == 1, "single-layer trunk"
    bsz, seq, d = x_prompt.shape
    nsmp = x_sample.shape[0]
    assert x_sample.shape[1] == 1 and d == D_MODEL
    past_len = page_table.shape[1] * PAGE_SIZE
    attn_w = N_HEADS * HEAD_DIM
    hd = (N_HEADS, HEAD_DIM)

    wl = w_in[0]
    cut = 2 * D_MODEL + 3 * attn_w
    w_all = jnp.concatenate(
        [wl[:, :cut], wl[:, cut + N_HEADS:], jnp.pad(wl[:, cut:cut + N_HEADS], ((0, 0), (0, LANES - N_HEADS)))],
        axis=1).astype(BF16)
    bf_pad = jnp.pad(b_f[0], (0, LANES - N_HEADS)).reshape(1, LANES)
    rg = (conv_w[0], conv_b[0].reshape(1, d), _pair_blocks(rg_wr[0]), _pair_blocks(rg_wi[0]),
          rg_br[0].reshape(1, d), rg_bi[0].reshape(1, d), rg_lambda[0].reshape(1, d))
    wa, wb, wo = (w[0].astype(BF16) for w in (w_proj_a, w_proj_b, w_o))
    wfi, wfo = w_ffn_in[0].astype(BF16), w_ffn_out[0].astype(BF16)
    n1, n2, nf = norm1_g[0].reshape(1, d), norm2_g[0].reshape(1, d), normf_g.reshape(1, d)

    mod = _ada(jnp.concatenate([c_prompt, c_sample], axis=0), ada_w[0], ada_b[0])
    mod_p = mod[:bsz].reshape(bsz, 1, 6 * d)
    mod_s = mod[bsz:].reshape(1, nsmp, 6 * d)

    xp = x_prompt.reshape(bsz * seq, d)
    xr, gr, k, v, ga, gb, lf, q_aug, k_aug, vt = _in_proj(xp, mod_p, seq, ROW_TILE, n1, w_all,
                                                         bf_pad, True)
    logf_p = lf[:, :N_HEADS].reshape(bsz, seq, N_HEADS)
    a_in, h_last_p = _rglru_seq(xr, gr, bsz, seq, rg)
    att = _fox_prompt(q_aug, k_aug, vt, bsz, seq)
    x1, h2 = _merge(xp, a_in, att, ga, gb, mod_p, seq, ROW_TILE, n2, wa, wb, wo)
    y_p = _ffn(x1, h2, mod_p, seq, ROW_TILE, nf, wfi, wfo).reshape(bsz, seq, d)
    conv_p = xr.reshape(bsz, seq, d)[:, seq - (CONV_W - 1):]

    xs = x_sample.reshape(nsmp, d)
    xr_s, gr_s, k_s, v_s, ga_s, gb_s, lf_s, q_s = _in_proj(xs, mod_s, nsmp, nsmp, n1, w_all,
                                                           bf_pad, False)
    logf_s = lf_s[:, :N_HEADS]
    a_in_s, h_s = _rglru_step(xr_s, gr_s, jnp.transpose(state_conv[0], (1, 0, 2)), state_rglru[0],
                              rg, past_len == 0)
    att_s = _fox_paged(q_s.reshape(nsmp, *hd), k_s.reshape(nsmp, *hd), v_s.reshape(nsmp, *hd),
                       logf_s.reshape(nsmp, N_HEADS, 1), cache_k, cache_v, cache_logf, page_table)
    x1_s, h2_s = _merge(xs, a_in_s, att_s.reshape(nsmp, d), ga_s, gb_s, mod_s, nsmp, nsmp, n2, wa, wb, wo)
    y_s = _ffn(x1_s, h2_s, mod_s, nsmp, nsmp, nf, wfi, wfo).reshape(nsmp, 1, d)
    conv_s = jnp.concatenate([state_conv[0][:, 1:], xr_s[:, None, :]], axis=1)

    return (y_p, y_s,
            k.reshape(1, bsz, seq, *hd), v.reshape(1, bsz, seq, *hd), logf_p[None],
            conv_p[None], h_last_p.reshape(1, bsz, d),
            k_s.reshape(1, nsmp, 1, *hd), v_s.reshape(1, nsmp, 1, *hd),
            logf_s.reshape(1, nsmp, 1, N_HEADS),
            conv_s[None], h_s[None])
```

```python
import functools

import numpy as np
import jax
import jax.numpy as jnp
from jax import lax
from jax.experimental import pallas as pl
from jax.experimental.pallas import tpu as pltpu

F32 = jnp.float32
BF16 = jnp.bfloat16

D_MODEL = 1024
N_HEADS = 16
HEAD_DIM = 64
N_BLK = 16
CONV_W = 4
RG_C = 8.0
PAGE_SIZE = 128
D_FF = 2816
EPS = 1e-6

LANES = 128
SUBLANES = 8
MXU_DIM = 256
VMEM_LIMIT = 56 * 1024 * 1024

ROW_TILE = 256
SCAN_TILE = 256
ATT_TILE = 512
ATT_HEADS = 4
PAGES_PER_STEP = 4

N_SEG = 7
W_ALL_COLS = N_SEG * D_MODEL + LANES

AUG = HEAD_DIM
N_SPLIT = 3
ONES_LANE = N_SPLIT * N_HEADS


def _params(*sem):
    return pltpu.CompilerParams(dimension_semantics=sem, vmem_limit_bytes=VMEM_LIMIT)


def _resident(shape):
    nd = len(shape)
    return pl.BlockSpec(shape, lambda *_: (0,) * nd, pipeline_mode=pl.Buffered(1))


def _sigmoid(x):
    return 1.0 / (1.0 + jnp.exp(-x))


def _log_sigmoid(x):
    return jnp.minimum(x, 0.0) - jnp.log1p(jnp.exp(-jnp.abs(x)))


def _softplus(x):
    return jnp.maximum(x, 0.0) + jnp.log1p(jnp.exp(-jnp.abs(x)))


def _rms(x, g):
    return x * lax.rsqrt(jnp.mean(x * x, axis=-1, keepdims=True) + EPS) * g


def _ada_kernel(c_ref, w_ref, b_ref, o_ref):
    c = c_ref[...]
    s = (c * _sigmoid(c)).astype(BF16)
    o_ref[...] = jnp.dot(s, w_ref[...].astype(BF16), preferred_element_type=F32) + b_ref[...]


def _ada(c, w, b):
    n, d = c.shape
    cols = w.shape[1]
    return pl.pallas_call(
        _ada_kernel,
        grid=(cols // d,),
        in_specs=[pl.BlockSpec((n, d), lambda j: (0, 0)),
                  pl.BlockSpec((d, d), lambda j: (0, j)),
                  pl.BlockSpec((1, d), lambda j: (0, j))],
        out_specs=pl.BlockSpec((n, d), lambda j: (0, j)),
        out_shape=jax.ShapeDtypeStruct((n, cols), F32),
        compiler_params=_params("arbitrary"),
        name="ada_modulation",
    )(c, w, b.reshape(1, cols))


def _mod_spec(rows_per_group, tm, r, chunk):
    tiles = rows_per_group // tm
    return pl.BlockSpec((1, r, D_MODEL), lambda i: (i // tiles, 0, chunk))


def _split3(x):
    hi = x.astype(BF16).astype(F32)
    r = x - hi
    mid = r.astype(BF16).astype(F32)
    return hi, mid, r - mid


def _in_proj_kernel(*refs, prompt, tiles_per_seq):
    if prompt:
        (x_ref, g_ref, sh_ref, sc_ref, w_ref, bf_ref, tri_ref, pq_ref, pk_ref,
         xr_ref, gr_ref, ga_ref, gb_ref, lf_ref, qa_ref, ka_ref, kt_ref, vt_ref, vtb_ref, fcar) = refs
    else:
        (x_ref, g_ref, sh_ref, sc_ref, w_ref, bf_ref,
         xr_ref, gr_ref, ga_ref, gb_ref, lf_ref, q_ref, k_ref, v_ref) = refs
    x = x_ref[...]
    h = (_rms(x, g_ref[...]) * (1.0 + sc_ref[0]) + sh_ref[0]).astype(BF16)

    def seg(s):
        return jnp.dot(h, w_ref[:, s * D_MODEL:(s + 1) * D_MODEL], preferred_element_type=F32)

    xr_ref[...] = seg(0)
    gr_ref[...] = seg(1)
    q = (seg(2) * (HEAD_DIM ** -0.5)).astype(BF16)
    k = seg(3)
    v = seg(4)
    ga_ref[...] = seg(5)
    gb_ref[...] = seg(6)
    fl = jnp.dot(h, w_ref[:, N_SEG * D_MODEL:], preferred_element_type=F32)
    lf = _log_sigmoid(fl + bf_ref[...])
    lf_ref[...] = lf
    if not prompt:
        q_ref[...] = q
        k_ref[...] = k
        v_ref[...] = v
        return

    tm = x.shape[0]
    kt_ref[0] = k.T
    vt = v.T
    vt_ref[0] = vt
    vtb_ref[0] = vt.astype(BF16)

    @pl.when(pl.program_id(0) % tiles_per_seq == 0)
    def _():
        fcar[...] = jnp.zeros_like(fcar)

    lane = lax.broadcasted_iota(jnp.int32, (1, LANES), 1)
    tri = tri_ref[...]
    cum = fcar[...]
    for piece in _split3(jnp.where(lane < N_HEADS, lf, 0.0)):
        cum = cum + jnp.dot(tri, piece.astype(BF16), preferred_element_type=F32)
    fcar[...] = cum[tm - 1:tm, :]

    c_hi, c_mid, c_lo = _split3(cum)
    faug = jnp.where(lane < N_HEADS, c_hi,
                     jnp.where(lane < 2 * N_HEADS, pltpu.roll(c_mid, N_HEADS, axis=1),
                               jnp.where(lane < ONES_LANE, pltpu.roll(c_lo, 2 * N_HEADS, axis=1),
                                         jnp.where(lane == ONES_LANE, 1.0, 0.0)))).astype(BF16)
    kb = k.astype(BF16)
    for j in range(N_HEADS // 2):
        pair = slice(j * LANES, (j + 1) * LANES)
        out = slice(j * MXU_DIM, (j + 1) * MXU_DIM)
        qa_ref[:, out] = jnp.dot(jnp.concatenate([q[:, pair], faug], axis=1), pq_ref[j],
                                 preferred_element_type=F32).astype(BF16)
        ka_ref[:, out] = jnp.dot(jnp.concatenate([kb[:, pair], faug], axis=1), pk_ref[j],
                                 preferred_element_type=F32).astype(BF16)


def _placement_matrices():
    n_pairs = N_HEADS // 2
    pq = np.zeros((n_pairs, MXU_DIM, MXU_DIM), np.float32)
    pk = np.zeros((n_pairs, MXU_DIM, MXU_DIM), np.float32)
    for j in range(n_pairs):
        for s in range(2):
            head = 2 * j + s
            for d in range(HEAD_DIM):
                pq[j, s * HEAD_DIM + d, s * LANES + d] = 1.0
                pk[j, s * HEAD_DIM + d, s * LANES + d] = 1.0
            cb = s * LANES + AUG
            for i in range(N_SPLIT):
                pq[j, LANES + i * N_HEADS + head, cb + i] = 1.0
                pq[j, LANES + ONES_LANE, cb + N_SPLIT + i] = 1.0
                pk[j, LANES + ONES_LANE, cb + i] = 1.0
                pk[j, LANES + i * N_HEADS + head, cb + N_SPLIT + i] = -1.0
    return jnp.asarray(pq, BF16), jnp.asarray(pk, BF16)


def _in_proj(x, mod, rows_per_group, tm, norm_g, w_all, bf_pad, prompt):
    m = x.shape[0]
    r = mod.shape[1]
    row = lambda i: (i, 0)
    f32o = jax.ShapeDtypeStruct((m, D_MODEL), F32)
    blk = pl.BlockSpec((tm, D_MODEL), row)
    in_specs = [blk, _resident((1, D_MODEL)),
                _mod_spec(rows_per_group, tm, r, 0), _mod_spec(rows_per_group, tm, r, 1),
                _resident((D_MODEL, W_ALL_COLS)), _resident((1, LANES))]
    operands = [x, norm_g, mod, mod, w_all, bf_pad]
    out_specs = [blk] * 4 + [pl.BlockSpec((tm, LANES), row)]
    out_shape = [f32o] * 4 + [jax.ShapeDtypeStruct((m, LANES), F32)]
    scratch = []
    if prompt:
        tiles = rows_per_group // tm
        n_seq = m // rows_per_group
        tri = jnp.asarray(np.tril(np.ones((tm, tm), np.float32)), BF16)
        pq, pk = _placement_matrices()
        in_specs += [_resident(tri.shape), _resident(pq.shape), _resident(pk.shape)]
        operands += [tri, pq, pk]
        wide = pl.BlockSpec((tm, 2 * D_MODEL), row)
        time_on_lanes = pl.BlockSpec((1, D_MODEL, tm), lambda i: (i // tiles, 0, i % tiles))
        out_specs += [wide, wide] + [time_on_lanes] * 3
        out_shape += [jax.ShapeDtypeStruct((m, 2 * D_MODEL), BF16)] * 2
        out_shape += [jax.ShapeDtypeStruct((n_seq, D_MODEL, rows_per_group), F32)] * 2
        out_shape += [jax.ShapeDtypeStruct((n_seq, D_MODEL, rows_per_group), BF16)]
        scratch = [pltpu.VMEM((1, LANES), F32)]
    else:
        out_specs += [blk] * 3
        out_shape += [jax.ShapeDtypeStruct((m, D_MODEL), BF16), f32o, f32o]
    return pl.pallas_call(
        functools.partial(_in_proj_kernel, prompt=prompt, tiles_per_seq=rows_per_group // tm),
        grid=(m // tm,),
        in_specs=in_specs, out_specs=out_specs, out_shape=out_shape, scratch_shapes=scratch,
        compiler_params=_params("arbitrary"),
        name="in_proj_prompt" if prompt else "in_proj_sample",
    )(*operands)


def _block_gates(xc, wr_ref, wi_ref, br, bi):
    xcb = xc.astype(BF16)
    n = D_MODEL // MXU_DIM
    pr = [jnp.dot(xcb[:, c * MXU_DIM:(c + 1) * MXU_DIM], wr_ref[c], preferred_element_type=F32)
          for c in range(n)]
    pi = [jnp.dot(xcb[:, c * MXU_DIM:(c + 1) * MXU_DIM], wi_ref[c], preferred_element_type=F32)
          for c in range(n)]
    gate_r = _sigmoid(jnp.concatenate(pr, axis=1) + br)
    gate_i = _sigmoid(jnp.concatenate(pi, axis=1) + bi)
    return gate_r, gate_i


def _decay_and_input(xc, gate_r, gate_i, lam, is_start):
    log_a = -RG_C * gate_r * _softplus(-lam)
    a = jnp.exp(log_a)
    mult = jnp.sqrt(jnp.tanh(-log_a) * (a * a + 1.0))
    if is_start is not None:
        mult = jnp.where(is_start, 1.0, mult)
    return a, mult * gate_i * xc


def _rglru_seq_kernel(xr_ref, gr_ref, cw_ref, cb_ref, wr_ref, wi_ref, br_ref, bi_ref, lam_ref,
                      o_ref, hl_ref, xbuf, a_s, u_s, hcar, *, tt):
    t = pl.program_id(1)
    seg_len = tt // SUBLANES

    @pl.when(t == 0)
    def _():
        xbuf[0:SUBLANES, :] = jnp.zeros((SUBLANES, D_MODEL), F32)
        hcar[...] = jnp.zeros_like(hcar)

    @pl.when(t != 0)
    def _():
        xbuf[0:SUBLANES, :] = xbuf[tt:tt + SUBLANES, :]

    xbuf[SUBLANES:, :] = xr_ref[...]
    xc = cb_ref[...] + cw_ref[CONV_W - 1:CONV_W, :] * xbuf[SUBLANES:, :]
    for j in range(CONV_W - 1):
        back = CONV_W - 1 - j
        xc = xc + cw_ref[j:j + 1, :] * xbuf[pl.ds(SUBLANES - back, tt), :]

    gate_r, gate_i = _block_gates(xc, wr_ref, wi_ref, br_ref[...], bi_ref[...])
    is_start = (lax.broadcasted_iota(jnp.int32, (tt, 1), 0) + t * tt) == 0
    a, u = _decay_and_input(xc, gate_r, gate_i, lam_ref[...], is_start)
    n_lane_chunks = D_MODEL // LANES
    for c in range(n_lane_chunks):
        a_s[c] = a[:, c * LANES:(c + 1) * LANES]
        u_s[c] = u[:, c * LANES:(c + 1) * LANES]

    for c in range(n_lane_chunks):
        hh = jnp.zeros((SUBLANES, LANES), F32)
        pp = jnp.ones((SUBLANES, LANES), F32)
        for r in range(seg_len):
            sl = pl.ds(r, SUBLANES, stride=seg_len)
            ar = a_s[c, sl, :]
            hh = ar * hh + u_s[c, sl, :]
            pp = ar * pp
            u_s[c, sl, :] = hh
            a_s[c, sl, :] = pp
        lanes = slice(c * LANES, (c + 1) * LANES)
        cy = hcar[0:1, lanes]
        carries = []
        for s in range(SUBLANES):
            carries.append(cy)
            cy = hh[s:s + 1, :] + pp[s:s + 1, :] * cy
        hcar[0:1, lanes] = cy
        hl_ref[0, :, lanes] = cy
        cin = jnp.concatenate(carries, axis=0)
        for r in range(seg_len):
            sl = pl.ds(r, SUBLANES, stride=seg_len)
            u_s[c, sl, :] = u_s[c, sl, :] + a_s[c, sl, :] * cin

    hr = jnp.concatenate([u_s[c] for c in range(n_lane_chunks)], axis=1)
    o_ref[...] = (hr * jax.nn.gelu(gr_ref[...])).astype(BF16)


def _rglru_seq(xr, gr, n_batch, seq, rg):
    tt = SCAN_TILE
    nt = seq // tt
    blk = pl.BlockSpec((tt, D_MODEL), lambda b, t: (b * nt + t, 0))
    nchunk = D_MODEL // MXU_DIM
    return pl.pallas_call(
        functools.partial(_rglru_seq_kernel, tt=tt),
        grid=(n_batch, nt),
        in_specs=[blk, blk, _resident((CONV_W, D_MODEL)), _resident((1, D_MODEL)),
                  _resident((nchunk, MXU_DIM, MXU_DIM)), _resident((nchunk, MXU_DIM, MXU_DIM)),
                  _resident((1, D_MODEL)), _resident((1, D_MODEL)), _resident((1, D_MODEL))],
        out_specs=[blk, pl.BlockSpec((1, 1, D_MODEL), lambda b, t: (b, 0, 0))],
        out_shape=[jax.ShapeDtypeStruct((n_batch * seq, D_MODEL), BF16),
                   jax.ShapeDtypeStruct((n_batch, 1, D_MODEL), F32)],
        scratch_shapes=[pltpu.VMEM((tt + SUBLANES, D_MODEL), F32),
                        pltpu.VMEM((D_MODEL // LANES, tt, LANES), F32),
                        pltpu.VMEM((D_MODEL // LANES, tt, LANES), F32),
                        pltpu.VMEM((SUBLANES, D_MODEL), F32)],
        compiler_params=_params("arbitrary", "arbitrary"),
        name="rglru_prompt",
    )(xr, gr, *rg)


def _rglru_step_kernel(xr_ref, gr_ref, st_ref, h0_ref, cw_ref, cb_ref, wr_ref, wi_ref, br_ref,
                       bi_ref, lam_ref, o_ref, h_ref, *, at_start):
    xr = xr_ref[...]
    xc = cb_ref[...] + cw_ref[CONV_W - 1:CONV_W, :] * xr
    for j in range(CONV_W - 1):
        xc = xc + cw_ref[j:j + 1, :] * st_ref[j]
    gate_r, gate_i = _block_gates(xc, wr_ref, wi_ref, br_ref[...], bi_ref[...])
    a, u = _decay_and_input(xc, gate_r, gate_i, lam_ref[...], True if at_start else None)
    h = a * h0_ref[...] + u
    h_ref[...] = h
    o_ref[...] = (h * jax.nn.gelu(gr_ref[...])).astype(BF16)


def _rglru_step(xr, gr, state_t, h0, rg, at_start):
    n = xr.shape[0]
    full = lambda shape: pl.BlockSpec(shape, lambda i: (0,) * len(shape))
    nchunk = D_MODEL // MXU_DIM
    return pl.pallas_call(
        functools.partial(_rglru_step_kernel, at_start=at_start),
        grid=(1,),
        in_specs=[full((n, D_MODEL)), full((n, D_MODEL)), full((CONV_W - 1, n, D_MODEL)),
                  full((n, D_MODEL)), full((CONV_W, D_MODEL)), full((1, D_MODEL)),
                  full((nchunk, MXU_DIM, MXU_DIM)), full((nchunk, MXU_DIM, MXU_DIM)),
                  full((1, D_MODEL)), full((1, D_MODEL)), full((1, D_MODEL))],
        out_specs=[full((n, D_MODEL)), full((n, D_MODEL))],
        out_shape=[jax.ShapeDtypeStruct((n, D_MODEL), BF16),
                   jax.ShapeDtypeStruct((n, D_MODEL), F32)],
        compiler_params=_params("arbitrary"),
        name="rglru_sample",
    )(xr, gr, state_t, h0, *rg)


def _fox_prompt_kernel(qt_ref, kt_ref, q_ref, k_ref, vt_ref, o_ref, m_s, l_s, acc_s, *, tq, tk):
    t = pl.program_id(2)
    qi = qt_ref[t]
    ki = kt_ref[t]

    @pl.when(ki == 0)
    def _():
        m_s[...] = jnp.full_like(m_s, -jnp.inf)
        l_s[...] = jnp.zeros_like(l_s)
        acc_s[...] = jnp.zeros_like(acc_s)

    def update(masked):
        for h in range(ATT_HEADS):
            lanes = slice(h * LANES, (h + 1) * LANES)
            st = lax.dot_general(k_ref[:, lanes], q_ref[:, lanes], (((1,), (1,)), ((), ())),
                                 preferred_element_type=F32)
            if masked:
                kpos = lax.broadcasted_iota(jnp.int32, (tk, tq), 0)
                qpos = lax.broadcasted_iota(jnp.int32, (tk, tq), 1)
                st = jnp.where(kpos <= qpos, st, -jnp.inf)
            m_prev = m_s[h]
            m_new = jnp.maximum(m_prev, jnp.max(st, axis=0, keepdims=True))
            alpha = jnp.exp(m_prev - m_new)
            p = jnp.exp(st - m_new)
            l_s[h] = alpha * l_s[h] + jnp.sum(p, axis=0, keepdims=True)
            pv = jnp.dot(vt_ref[0, h * HEAD_DIM:(h + 1) * HEAD_DIM, :], p.astype(BF16),
                         preferred_element_type=F32)
            acc_s[h] = alpha * acc_s[h] + pv
            m_s[h] = m_new

    @pl.when(ki < qi)
    def _():
        update(False)

    @pl.when(ki == qi)
    def _():
        update(True)
        ot = jnp.concatenate([acc_s[h] / l_s[h] for h in range(ATT_HEADS)], axis=0)
        o_ref[...] = ot.T.astype(BF16)


def _fox_prompt(q_aug, k_aug, vt, n_batch, seq):
    tq = tk = ATT_TILE
    nq = seq // tq
    pairs = [(i, j) for i in range(nq) for j in range(i + 1)]
    qt = jnp.asarray(np.array([p[0] for p in pairs], np.int32))
    kt = jnp.asarray(np.array([p[1] for p in pairs], np.int32))
    groups = N_HEADS // ATT_HEADS
    wide = ATT_HEADS * LANES
    narrow = ATT_HEADS * HEAD_DIM
    grid_spec = pltpu.PrefetchScalarGridSpec(
        num_scalar_prefetch=2,
        grid=(n_batch, groups, len(pairs)),
        in_specs=[
            pl.BlockSpec((tq, wide), lambda b, g, t, qt, kt: (b * nq + qt[t], g)),
            pl.BlockSpec((tk, wide), lambda b, g, t, qt, kt: (b * nq + kt[t], g)),
            pl.BlockSpec((1, narrow, tk), lambda b, g, t, qt, kt: (b, g, kt[t])),
        ],
        out_specs=pl.BlockSpec((tq, narrow), lambda b, g, t, qt, kt: (b * nq + qt[t], g)),
        scratch_shapes=[pltpu.VMEM((ATT_HEADS, 1, tq), F32), pltpu.VMEM((ATT_HEADS, 1, tq), F32),
                        pltpu.VMEM((ATT_HEADS, HEAD_DIM, tq), F32)],
    )
    return pl.pallas_call(
        functools.partial(_fox_prompt_kernel, tq=tq, tk=tk),
        grid_spec=grid_spec,
        out_shape=jax.ShapeDtypeStruct((n_batch * seq, D_MODEL), BF16),
        compiler_params=_params("arbitrary", "arbitrary", "arbitrary"),
        name="fox_prompt",
    )(qt, kt, q_aug, k_aug, vt)


def _fox_paged_kernel(pt_ref, qr_ref, qc_ref, kc_ref, vc_ref, lc_ref, *rest, n_steps, g):
    k_refs = rest[0:g]
    v_refs = rest[g:2 * g]
    lf_refs = rest[2 * g:3 * g]
    o_ref = rest[3 * g]
    qb_s, m_s, l_s, acc_s, car_s = rest[3 * g + 1:]
    p = pl.program_id(1)

    @pl.when(p == 0)
    def _():
        qb_s[...] = jnp.broadcast_to(qc_ref[0].astype(F32), qb_s.shape)
        m_s[...] = jnp.full_like(m_s, -jnp.inf)
        l_s[...] = jnp.zeros_like(l_s)
        acc_s[...] = jnp.zeros_like(acc_s)
        car_s[...] = jnp.zeros_like(car_s)

    lane = lax.broadcasted_iota(jnp.int32, (N_HEADS, PAGE_SIZE), 1)

    carry = car_s[...]
    logits = []
    for j in range(g):
        rows = [jnp.sum(k_refs[j][0, 0, h] * qb_s[h], axis=0, keepdims=True)
                for h in range(N_HEADS)]
        pref = lf_refs[j][0, 0]
        d = 1
        while d < PAGE_SIZE:
            pref = pref + jnp.where(lane >= d, pltpu.roll(pref, d, axis=1), 0.0)
            d *= 2
        logits.append(jnp.concatenate(rows, axis=0) - (carry + pref))
        carry = carry + pref[:, PAGE_SIZE - 1:PAGE_SIZE]
    car_s[...] = carry
    s = jnp.concatenate(logits, axis=1)
    m_prev = m_s[...]
    m_new = jnp.maximum(m_prev, jnp.max(s, axis=1, keepdims=True))
    alpha = jnp.exp(m_prev - m_new)
    pr = jnp.exp(s - m_new)
    l_new = alpha * l_s[...] + jnp.sum(pr, axis=1, keepdims=True)
    l_s[...] = l_new
    m_s[...] = m_new
    for h in range(N_HEADS):
        a_h = acc_s[h] * alpha[h:h + 1, :]
        for j in range(g):
            a_h = a_h + pr[h:h + 1, j * PAGE_SIZE:(j + 1) * PAGE_SIZE] * v_refs[j][0, 0, h]
        acc_s[h] = a_h

    @pl.when(p == n_steps - 1)
    def _():
        s_c = jnp.sum(qr_ref[0].astype(F32) * kc_ref[0], axis=1, keepdims=True) - (carry + lc_ref[0])
        m_f = jnp.maximum(m_new, s_c)
        al = jnp.exp(m_new - m_f)
        p_c = jnp.exp(s_c - m_f)
        inv_l = 1.0 / (al * l_new + p_c)
        for h in range(N_HEADS):
            past = jnp.sum(acc_s[h], axis=1, keepdims=True) * al[h:h + 1, :]
            o_ref[0, h] = (past + p_c[h:h + 1, :] * vc_ref[0, h]) * inv_l[h:h + 1, :]


def _fox_paged(q, k_cur, v_cur, lf_cur, cache_kt, cache_vt, cache_lft, page_table):
    n, n_pages = page_table.shape
    g = PAGES_PER_STEP
    n_steps = n_pages // g
    row = pl.BlockSpec((1, N_HEADS, HEAD_DIM), lambda b, p, pt: (b, 0, 0))
    col = pl.BlockSpec((1, N_HEADS, HEAD_DIM, 1), lambda b, p, pt: (b, 0, 0, 0))

    def page(tail, j):
        zeros = (0,) * len(tail)
        return pl.BlockSpec((1, 1) + tail, lambda b, p, pt: (0, pt[b, p * g + j]) + zeros)

    grid_spec = pltpu.PrefetchScalarGridSpec(
        num_scalar_prefetch=1,
        grid=(n, n_steps),
        in_specs=([row, col, row, col, pl.BlockSpec((1, N_HEADS, 1), lambda b, p, pt: (b, 0, 0))]
                  + [page((N_HEADS, HEAD_DIM, PAGE_SIZE), j) for j in range(g)]
                  + [page((N_HEADS, HEAD_DIM, PAGE_SIZE), j) for j in range(g)]
                  + [page((N_HEADS, PAGE_SIZE), j) for j in range(g)]),
        out_specs=col,
        scratch_shapes=[pltpu.VMEM((N_HEADS, HEAD_DIM, PAGE_SIZE), F32),
                        pltpu.VMEM((N_HEADS, 1), F32), pltpu.VMEM((N_HEADS, 1), F32),
                        pltpu.VMEM((N_HEADS, HEAD_DIM, PAGE_SIZE), F32),
                        pltpu.VMEM((N_HEADS, 1), F32)],
    )
    return pl.pallas_call(
        functools.partial(_fox_paged_kernel, n_steps=n_steps, g=g),
        grid_spec=grid_spec,
        out_shape=jax.ShapeDtypeStruct((n, N_HEADS, HEAD_DIM, 1), F32),
        compiler_params=_params("arbitrary", "arbitrary"),
        name="fox_paged",
    )(page_table, q, q[..., None], k_cur, v_cur[..., None], lf_cur,
      *([cache_kt] * g), *([cache_vt] * g), *([cache_lft] * g))


def _merge_kernel(x_ref, a_ref, b_ref, ga_ref, gb_ref, g1_ref, sh_ref, sc_ref, n2_ref,
                  wa_ref, wb_ref, wo_ref, x1_ref, h2_ref):
    ya = jnp.dot(a_ref[...], wa_ref[...], preferred_element_type=F32)
    yb = jnp.dot(b_ref[...], wb_ref[...], preferred_element_type=F32)
    mix = (_sigmoid(ga_ref[...]) * ya + _sigmoid(gb_ref[...]) * yb).astype(BF16)
    x1 = x_ref[...] + g1_ref[0] * jnp.dot(mix, wo_ref[...], preferred_element_type=F32)
    x1_ref[...] = x1
    h2_ref[...] = (_rms(x1, n2_ref[...]) * (1.0 + sc_ref[0]) + sh_ref[0]).astype(BF16)


def _merge(x, a_in, b_in, ga, gb, mod, rows_per_group, tm, norm_g, wa, wb, wo):
    m = x.shape[0]
    r = mod.shape[1]
    blk = pl.BlockSpec((tm, D_MODEL), lambda i: (i, 0))
    sq = _resident((D_MODEL, D_MODEL))
    return pl.pallas_call(
        _merge_kernel,
        grid=(m // tm,),
        in_specs=[blk, blk, blk, blk, blk,
                  _mod_spec(rows_per_group, tm, r, 2), _mod_spec(rows_per_group, tm, r, 3),
                  _mod_spec(rows_per_group, tm, r, 4), _resident((1, D_MODEL)), sq, sq, sq],
        out_specs=[blk, blk],
        out_shape=[jax.ShapeDtypeStruct((m, D_MODEL), F32), jax.ShapeDtypeStruct((m, D_MODEL), BF16)],
        compiler_params=_params("arbitrary"),
        name="merge_proj",
    )(x, a_in, b_in, ga, gb, mod, mod, mod, norm_g, wa, wb, wo)


def _ffn_kernel(x1_ref, h2_ref, g2_ref, nf_ref, wi_ref, wo_ref, y_ref):
    h2 = h2_ref[...]
    gu = jnp.dot(h2, wi_ref[:, 0:D_FF], preferred_element_type=F32)
    up = jnp.dot(h2, wi_ref[:, D_FF:2 * D_FF], preferred_element_type=F32)
    act = (gu * _sigmoid(gu) * up).astype(BF16)
    x2 = x1_ref[...] + g2_ref[0] * jnp.dot(act, wo_ref[...], preferred_element_type=F32)
    y_ref[...] = _rms(x2, nf_ref[...])


def _ffn(x1, h2, mod, rows_per_group, tm, normf_g, w_in, w_out):
    m = x1.shape[0]
    r = mod.shape[1]
    blk = pl.BlockSpec((tm, D_MODEL), lambda i: (i, 0))
    return pl.pallas_call(
        _ffn_kernel,
        grid=(m // tm,),
        in_specs=[blk, blk, _mod_spec(rows_per_group, tm, r, 5), _resident((1, D_MODEL)),
                  _resident((D_MODEL, 2 * D_FF)), _resident((D_FF, D_MODEL))],
        out_specs=blk,
        out_shape=jax.ShapeDtypeStruct((m, D_MODEL), F32),
        compiler_params=_params("arbitrary"),
        name="ffn_final",
    )(x1, h2, mod, normf_g, w_in, w_out)


def _pair_blocks(w):
    per = MXU_DIM // (D_MODEL // N_BLK)
    bw = D_MODEL // N_BLK
    w = w.reshape(N_BLK // per, per, bw, bw)
    eye = jnp.eye(per, dtype=w.dtype)
    return jnp.einsum("cpij,pq->cpiqj", w, eye).reshape(N_BLK // per, MXU_DIM, MXU_DIM).astype(BF16)


def kernel(x_prompt, x_sample, c_prompt, c_sample, cache_k, cache_v, cache_logf, state_conv,
           state_rglru, page_table, ada_w, ada_b, norm1_g, norm2_g, normf_g, w_in, b_f, conv_w,
           conv_b, rg_wr, rg_br, rg_wi, rg_bi, rg_lambda, w_proj_a, w_proj_b, w_o, w_ffn_in,
           w_ffn_out):
    depth = ada_w.shape[0]
    assert depth == 1, "single-layer trunk"
    bsz, seq, d = x_prompt.shape
    nsmp = x_sample.shape[0]
    assert x_sample.shape[1] == 1 and d == D_MODEL
    past_len = page_table.shape[1] * PAGE_SIZE
    attn_w = N_HEADS * HEAD_DIM
    hd = (N_HEADS, HEAD_DIM)

    wl = w_in[0]
    cut = 2 * D_MODEL + 3 * attn_w
    w_all = jnp.concatenate(
        [wl[:, :cut], wl[:, cut + N_HEADS:], jnp.pad(wl[:, cut:cut + N_HEADS], ((0, 0), (0, LANES - N_HEADS)))],
        axis=1).astype(BF16)
    bf_pad = jnp.pad(b_f[0], (0, LANES - N_HEADS)).reshape(1, LANES)
    rg = (conv_w[0], conv_b[0].reshape(1, d), _pair_blocks(rg_wr[0]), _pair_blocks(rg_wi[0]),
          rg_br[0].reshape(1, d), rg_bi[0].reshape(1, d), rg_lambda[0].reshape(1, d))
    wa, wb, wo = (w[0].astype(BF16) for w in (w_proj_a, w_proj_b, w_o))
    wfi, wfo = w_ffn_in[0].astype(BF16), w_ffn_out[0].astype(BF16)
    n1, n2, nf = norm1_g[0].reshape(1, d), norm2_g[0].reshape(1, d), normf_g.reshape(1, d)

    mod = _ada(jnp.concatenate([c_prompt, c_sample], axis=0), ada_w[0], ada_b[0])
    mod_p = mod[:bsz].reshape(bsz, 1, 6 * d)
    mod_s = mod[bsz:].reshape(1, nsmp, 6 * d)

    xp = x_prompt.reshape(bsz * seq, d)
    xr, gr, ga, gb, lf, q_aug, k_aug, kt, vt, vtb = _in_proj(xp, mod_p, seq, ROW_TILE, n1, w_all,
                                                             bf_pad, True)
    logf_p = lf[:, :N_HEADS].reshape(bsz, seq, N_HEADS)
    a_in, h_last_p = _rglru_seq(xr, gr, bsz, seq, rg)
    att = _fox_prompt(q_aug, k_aug, vtb, bsz, seq)
    k_p = jnp.transpose(kt.reshape(1, bsz, *hd, seq), (0, 1, 4, 2, 3))
    v_p = jnp.transpose(vt.reshape(1, bsz, *hd, seq), (0, 1, 4, 2, 3))
    x1, h2 = _merge(xp, a_in, att, ga, gb, mod_p, seq, ROW_TILE, n2, wa, wb, wo)
    y_p = _ffn(x1, h2, mod_p, seq, ROW_TILE, nf, wfi, wfo).reshape(bsz, seq, d)
    conv_p = xr.reshape(bsz, seq, d)[:, seq - (CONV_W - 1):]

    xs = x_sample.reshape(nsmp, d)
    xr_s, gr_s, ga_s, gb_s, lf_s, q_s, k_s, v_s = _in_proj(xs, mod_s, nsmp, nsmp, n1, w_all,
                                                           bf_pad, False)
    logf_s = lf_s[:, :N_HEADS]
    a_in_s, h_s = _rglru_step(xr_s, gr_s, jnp.transpose(state_conv[0], (1, 0, 2)), state_rglru[0],
                              rg, past_len == 0)
    att_s = _fox_paged(q_s.astype(F32).reshape(nsmp, *hd), k_s.reshape(nsmp, *hd),
                       v_s.reshape(nsmp, *hd), logf_s.reshape(nsmp, N_HEADS, 1),
                       jnp.transpose(cache_k, (0, 1, 3, 4, 2)), jnp.transpose(cache_v, (0, 1, 3, 4, 2)),
                       jnp.transpose(cache_logf, (0, 1, 3, 2)), page_table)
    x1_s, h2_s = _merge(xs, a_in_s, att_s.reshape(nsmp, d).astype(BF16), ga_s, gb_s, mod_s, nsmp,
                        nsmp, n2, wa, wb, wo)
    y_s = _ffn(x1_s, h2_s, mod_s, nsmp, nsmp, nf, wfi, wfo).reshape(nsmp, 1, d)
    conv_s = jnp.concatenate([state_conv[0][:, 1:], xr_s[:, None, :]], axis=1)

    return (y_p, y_s,
            k_p, v_p, logf_p[None],
            conv_p[None], h_last_p.reshape(1, bsz, d),
            k_s.reshape(1, nsmp, 1, *hd), v_s.reshape(1, nsmp, 1, *hd),
            logf_s.reshape(1, nsmp, 1, N_HEADS),
            conv_s[None], h_s[None])
```

```python
import functools

import numpy as np
import jax
import jax.numpy as jnp
from jax import lax
from jax.experimental import pallas as pl
from jax.experimental.pallas import tpu as pltpu

F32 = jnp.float32
BF16 = jnp.bfloat16

D_MODEL = 1024
N_HEADS = 16
HEAD_DIM = 64
N_BLK = 16
CONV_W = 4
RG_C = 8.0
PAGE_SIZE = 128
D_FF = 2816
EPS = 1e-6

LANES = 128
SUBLANES = 8
MXU_DIM = 256
VMEM_LIMIT = 56 * 1024 * 1024

ROW_TILE = 256
SCAN_TILE = 256
ATT_TILE = 512
ATT_HEADS = 8
PAGES_PER_STEP = 8

N_SEG = 7
W_ALL_COLS = N_SEG * D_MODEL + LANES

AUG = HEAD_DIM
N_SPLIT = 3
ONES_LANE = N_SPLIT * N_HEADS


def _params(*sem):
    return pltpu.CompilerParams(dimension_semantics=sem, vmem_limit_bytes=VMEM_LIMIT)


def _resident(shape):
    nd = len(shape)
    return pl.BlockSpec(shape, lambda *_: (0,) * nd, pipeline_mode=pl.Buffered(1))


def _sigmoid(x):
    return 1.0 / (1.0 + jnp.exp(-x))


def _log_sigmoid(x):
    return jnp.minimum(x, 0.0) - jnp.log1p(jnp.exp(-jnp.abs(x)))


def _softplus(x):
    return jnp.maximum(x, 0.0) + jnp.log1p(jnp.exp(-jnp.abs(x)))


def _rms(x, g):
    return x * lax.rsqrt(jnp.mean(x * x, axis=-1, keepdims=True) + EPS) * g


def _ada_kernel(c_ref, w_ref, b_ref, o_ref):
    c = c_ref[...]
    s = (c * _sigmoid(c)).astype(BF16)
    o_ref[...] = jnp.dot(s, w_ref[...].astype(BF16), preferred_element_type=F32) + b_ref[...]


def _ada(c, w, b):
    n, d = c.shape
    cols = w.shape[1]
    return pl.pallas_call(
        _ada_kernel,
        grid=(cols // d,),
        in_specs=[pl.BlockSpec((n, d), lambda j: (0, 0)),
                  pl.BlockSpec((d, d), lambda j: (0, j)),
                  pl.BlockSpec((1, d), lambda j: (0, j))],
        out_specs=pl.BlockSpec((n, d), lambda j: (0, j)),
        out_shape=jax.ShapeDtypeStruct((n, cols), F32),
        compiler_params=_params("arbitrary"),
        name="ada_modulation",
    )(c, w, b.reshape(1, cols))


def _mod_spec(rows_per_group, tm, r, chunk):
    tiles = rows_per_group // tm
    return pl.BlockSpec((1, r, D_MODEL), lambda i: (i // tiles, 0, chunk))


def _split3(x):
    hi = x.astype(BF16).astype(F32)
    r = x - hi
    mid = r.astype(BF16).astype(F32)
    return hi, mid, r - mid


def _in_proj_kernel(*refs, prompt, tiles_per_seq):
    if prompt:
        (x_ref, g_ref, sh_ref, sc_ref, w_ref, bf_ref, tri_ref, pq_ref, pk_ref, perm_ref,
         xr_ref, gr_ref, ga_ref, gb_ref, lf_ref, qa_ref, ka_ref, kt_ref, vt_ref, vtb_ref, fcar) = refs
    else:
        (x_ref, g_ref, sh_ref, sc_ref, w_ref, bf_ref,
         xr_ref, gr_ref, ga_ref, gb_ref, lf_ref, q_ref, k_ref, v_ref) = refs
    x = x_ref[...]
    h = (_rms(x, g_ref[...]) * (1.0 + sc_ref[0]) + sh_ref[0]).astype(BF16)

    def seg(s, lhs=h):
        return jnp.dot(lhs, w_ref[:, s * D_MODEL:(s + 1) * D_MODEL], preferred_element_type=F32)

    if prompt:
        hs = jnp.dot(perm_ref[...], h, preferred_element_type=F32).astype(BF16)
        xr_ref[...] = seg(0, hs)
        gr_ref[...] = seg(1, hs)
    else:
        xr_ref[...] = seg(0)
        gr_ref[...] = seg(1)
    q = (seg(2) * (HEAD_DIM ** -0.5)).astype(BF16)
    k = seg(3)
    v = seg(4)
    ga_ref[...] = seg(5)
    gb_ref[...] = seg(6)
    fl = jnp.dot(h, w_ref[:, N_SEG * D_MODEL:], preferred_element_type=F32)
    lf = _log_sigmoid(fl + bf_ref[...])
    lf_ref[...] = lf
    if not prompt:
        q_ref[...] = q
        k_ref[...] = k
        v_ref[...] = v
        return

    tm = x.shape[0]
    kt_ref[0] = k.T
    vt = v.T
    vt_ref[0] = vt
    vtb_ref[0] = vt.astype(BF16)

    @pl.when(pl.program_id(0) % tiles_per_seq == 0)
    def _():
        fcar[...] = jnp.zeros_like(fcar)

    lane = lax.broadcasted_iota(jnp.int32, (1, LANES), 1)
    tri = tri_ref[...]
    cum = fcar[...]
    for piece in _split3(jnp.where(lane < N_HEADS, lf, 0.0)):
        cum = cum + jnp.dot(tri, piece.astype(BF16), preferred_element_type=F32)
    fcar[...] = cum[tm - 1:tm, :]

    c_hi, c_mid, c_lo = _split3(cum)
    faug = jnp.where(lane < N_HEADS, c_hi,
                     jnp.where(lane < 2 * N_HEADS, pltpu.roll(c_mid, N_HEADS, axis=1),
                               jnp.where(lane < ONES_LANE, pltpu.roll(c_lo, 2 * N_HEADS, axis=1),
                                         jnp.where(lane == ONES_LANE, 1.0, 0.0)))).astype(BF16)
    kb = k.astype(BF16)
    for j in range(N_HEADS // 2):
        pair = slice(j * LANES, (j + 1) * LANES)
        out = slice(j * MXU_DIM, (j + 1) * MXU_DIM)
        qa_ref[:, out] = jnp.dot(jnp.concatenate([q[:, pair], faug], axis=1), pq_ref[j],
                                 preferred_element_type=F32).astype(BF16)
        ka_ref[:, out] = jnp.dot(jnp.concatenate([kb[:, pair], faug], axis=1), pk_ref[j],
                                 preferred_element_type=F32).astype(BF16)


def _scan_order(tile):
    seg_len = tile // SUBLANES
    n = np.arange(tile)
    perm = np.zeros((tile, tile), np.float32)
    perm[n, (n % SUBLANES) * seg_len + n // SUBLANES] = 1.0
    return perm


def _placement_matrices():
    n_pairs = N_HEADS // 2
    pq = np.zeros((n_pairs, MXU_DIM, MXU_DIM), np.float32)
    pk = np.zeros((n_pairs, MXU_DIM, MXU_DIM), np.float32)
    for j in range(n_pairs):
        for s in range(2):
            head = 2 * j + s
            for d in range(HEAD_DIM):
                pq[j, s * HEAD_DIM + d, s * LANES + d] = 1.0
                pk[j, s * HEAD_DIM + d, s * LANES + d] = 1.0
            cb = s * LANES + AUG
            for i in range(N_SPLIT):
                pq[j, LANES + i * N_HEADS + head, cb + i] = 1.0
                pq[j, LANES + ONES_LANE, cb + N_SPLIT + i] = 1.0
                pk[j, LANES + ONES_LANE, cb + i] = 1.0
                pk[j, LANES + i * N_HEADS + head, cb + N_SPLIT + i] = -1.0
    return jnp.asarray(pq, BF16), jnp.asarray(pk, BF16)


def _in_proj(x, mod, rows_per_group, tm, norm_g, w_all, bf_pad, prompt):
    m = x.shape[0]
    r = mod.shape[1]
    row = lambda i: (i, 0)
    f32o = jax.ShapeDtypeStruct((m, D_MODEL), F32)
    blk = pl.BlockSpec((tm, D_MODEL), row)
    in_specs = [blk, _resident((1, D_MODEL)),
                _mod_spec(rows_per_group, tm, r, 0), _mod_spec(rows_per_group, tm, r, 1),
                _resident((D_MODEL, W_ALL_COLS)), _resident((1, LANES))]
    operands = [x, norm_g, mod, mod, w_all, bf_pad]
    out_specs = [blk] * 4 + [pl.BlockSpec((tm, LANES), row)]
    out_shape = [f32o] * 4 + [jax.ShapeDtypeStruct((m, LANES), F32)]
    scratch = []
    if prompt:
        tiles = rows_per_group // tm
        n_seq = m // rows_per_group
        tri = jnp.asarray(np.tril(np.ones((tm, tm), np.float32)), BF16)
        pq, pk = _placement_matrices()
        perm = jnp.asarray(_scan_order(tm), BF16)
        in_specs += [_resident(tri.shape), _resident(pq.shape), _resident(pk.shape),
                     _resident(perm.shape)]
        operands += [tri, pq, pk, perm]
        wide = pl.BlockSpec((tm, 2 * D_MODEL), row)
        time_on_lanes = pl.BlockSpec((1, D_MODEL, tm), lambda i: (i // tiles, 0, i % tiles))
        out_specs += [wide, wide] + [time_on_lanes] * 3
        out_shape += [jax.ShapeDtypeStruct((m, 2 * D_MODEL), BF16)] * 2
        out_shape += [jax.ShapeDtypeStruct((n_seq, D_MODEL, rows_per_group), F32)] * 2
        out_shape += [jax.ShapeDtypeStruct((n_seq, D_MODEL, rows_per_group), BF16)]
        scratch = [pltpu.VMEM((1, LANES), F32)]
    else:
        out_specs += [blk] * 3
        out_shape += [jax.ShapeDtypeStruct((m, D_MODEL), BF16), f32o, f32o]
    return pl.pallas_call(
        functools.partial(_in_proj_kernel, prompt=prompt, tiles_per_seq=rows_per_group // tm),
        grid=(m // tm,),
        in_specs=in_specs, out_specs=out_specs, out_shape=out_shape, scratch_shapes=scratch,
        compiler_params=_params("arbitrary"),
        name="in_proj_prompt" if prompt else "in_proj_sample",
    )(*operands)


def _block_gates(xc, wr_ref, wi_ref, br, bi):
    xcb = xc.astype(BF16)
    n = D_MODEL // MXU_DIM
    pr = [jnp.dot(xcb[:, c * MXU_DIM:(c + 1) * MXU_DIM], wr_ref[c], preferred_element_type=F32)
          for c in range(n)]
    pi = [jnp.dot(xcb[:, c * MXU_DIM:(c + 1) * MXU_DIM], wi_ref[c], preferred_element_type=F32)
          for c in range(n)]
    gate_r = _sigmoid(jnp.concatenate(pr, axis=1) + br)
    gate_i = _sigmoid(jnp.concatenate(pi, axis=1) + bi)
    return gate_r, gate_i


def _decay_and_input(xc, gate_r, gate_i, lam, is_start):
    log_a = -RG_C * gate_r * _softplus(-lam)
    a = jnp.exp(log_a)
    mult = jnp.sqrt(jnp.tanh(-log_a) * (a * a + 1.0))
    if is_start is not None:
        mult = jnp.where(is_start, 1.0, mult)
    return a, mult * gate_i * xc


def _rglru_seq_kernel(xr_ref, gr_ref, unperm_ref, cw_ref, cb_ref, wr_ref, wi_ref, br_ref, bi_ref,
                      lam_ref, o_ref, hl_ref, xbuf, prev_s, hcar, *, tt):
    t = pl.program_id(1)
    seg_len = tt // SUBLANES
    halo = (CONV_W - 1) * SUBLANES

    @pl.when(t == 0)
    def _():
        prev_s[...] = jnp.zeros_like(prev_s)
        hcar[...] = jnp.zeros_like(hcar)

    x = xr_ref[...]
    sub = lax.broadcasted_iota(jnp.int32, (SUBLANES, 1), 0)
    for i in range(CONV_W - 1):
        rows = slice(i * SUBLANES, (i + 1) * SUBLANES)
        cur = x[tt - halo + i * SUBLANES:tt - halo + (i + 1) * SUBLANES, :]
        xbuf[rows, :] = jnp.where(sub == 0, pltpu.roll(prev_s[rows, :], 1, axis=0),
                                  pltpu.roll(cur, 1, axis=0))
    prev_s[...] = x[tt - halo:, :]
    xbuf[halo:, :] = x
    xc = cb_ref[...] + cw_ref[CONV_W - 1:CONV_W, :] * x
    for j in range(CONV_W - 1):
        back = CONV_W - 1 - j
        xc = xc + cw_ref[j:j + 1, :] * xbuf[pl.ds(halo - back * SUBLANES, tt), :]

    gate_r, gate_i = _block_gates(xc, wr_ref, wi_ref, br_ref[...], bi_ref[...])
    is_start = (lax.broadcasted_iota(jnp.int32, (tt, 1), 0) + t * tt) == 0
    a, u = _decay_and_input(xc, gate_r, gate_i, lam_ref[...], is_start)

    hh = jnp.zeros((SUBLANES, D_MODEL), F32)
    pp = jnp.ones((SUBLANES, D_MODEL), F32)
    local, decay = [], []
    for g in range(seg_len):
        rows = slice(g * SUBLANES, (g + 1) * SUBLANES)
        hh = a[rows, :] * hh + u[rows, :]
        pp = a[rows, :] * pp
        local.append(hh)
        decay.append(pp)
    cy = hcar[0:1, :]
    carries = []
    for s in range(SUBLANES):
        carries.append(cy)
        cy = hh[s:s + 1, :] + pp[s:s + 1, :] * cy
    hcar[0:1, :] = cy
    hl_ref[0] = cy
    cin = jnp.concatenate(carries, axis=0)
    hr = jnp.concatenate([local[g] + decay[g] * cin for g in range(seg_len)], axis=0)
    gated = (hr * jax.nn.gelu(gr_ref[...])).astype(BF16)
    o_ref[...] = jnp.dot(unperm_ref[...], gated, preferred_element_type=F32).astype(BF16)


def _rglru_seq(xr, gr, n_batch, seq, rg):
    tt = SCAN_TILE
    nt = seq // tt
    blk = pl.BlockSpec((tt, D_MODEL), lambda b, t: (b * nt + t, 0))
    nchunk = D_MODEL // MXU_DIM
    unperm = jnp.asarray(_scan_order(tt).T, BF16)
    return pl.pallas_call(
        functools.partial(_rglru_seq_kernel, tt=tt),
        grid=(n_batch, nt),
        in_specs=[blk, blk, _resident((tt, tt)), _resident((CONV_W, D_MODEL)),
                  _resident((1, D_MODEL)),
                  _resident((nchunk, MXU_DIM, MXU_DIM)), _resident((nchunk, MXU_DIM, MXU_DIM)),
                  _resident((1, D_MODEL)), _resident((1, D_MODEL)), _resident((1, D_MODEL))],
        out_specs=[blk, pl.BlockSpec((1, 1, D_MODEL), lambda b, t: (b, 0, 0))],
        out_shape=[jax.ShapeDtypeStruct((n_batch * seq, D_MODEL), BF16),
                   jax.ShapeDtypeStruct((n_batch, 1, D_MODEL), F32)],
        scratch_shapes=[pltpu.VMEM((tt + (CONV_W - 1) * SUBLANES, D_MODEL), F32),
                        pltpu.VMEM(((CONV_W - 1) * SUBLANES, D_MODEL), F32),
                        pltpu.VMEM((SUBLANES, D_MODEL), F32)],
        compiler_params=_params("arbitrary", "arbitrary"),
        name="rglru_prompt",
    )(xr, gr, unperm, *rg)


def _rglru_step_kernel(xr_ref, gr_ref, st_ref, h0_ref, cw_ref, cb_ref, wr_ref, wi_ref, br_ref,
                       bi_ref, lam_ref, o_ref, h_ref, *, at_start):
    xr = xr_ref[...]
    xc = cb_ref[...] + cw_ref[CONV_W - 1:CONV_W, :] * xr
    for j in range(CONV_W - 1):
        xc = xc + cw_ref[j:j + 1, :] * st_ref[j]
    gate_r, gate_i = _block_gates(xc, wr_ref, wi_ref, br_ref[...], bi_ref[...])
    a, u = _decay_and_input(xc, gate_r, gate_i, lam_ref[...], True if at_start else None)
    h = a * h0_ref[...] + u
    h_ref[...] = h
    o_ref[...] = (h * jax.nn.gelu(gr_ref[...])).astype(BF16)


def _rglru_step(xr, gr, state_t, h0, rg, at_start):
    n = xr.shape[0]
    full = lambda shape: pl.BlockSpec(shape, lambda i: (0,) * len(shape))
    nchunk = D_MODEL // MXU_DIM
    return pl.pallas_call(
        functools.partial(_rglru_step_kernel, at_start=at_start),
        grid=(1,),
        in_specs=[full((n, D_MODEL)), full((n, D_MODEL)), full((CONV_W - 1, n, D_MODEL)),
                  full((n, D_MODEL)), full((CONV_W, D_MODEL)), full((1, D_MODEL)),
                  full((nchunk, MXU_DIM, MXU_DIM)), full((nchunk, MXU_DIM, MXU_DIM)),
                  full((1, D_MODEL)), full((1, D_MODEL)), full((1, D_MODEL))],
        out_specs=[full((n, D_MODEL)), full((n, D_MODEL))],
        out_shape=[jax.ShapeDtypeStruct((n, D_MODEL), BF16),
                   jax.ShapeDtypeStruct((n, D_MODEL), F32)],
        compiler_params=_params("arbitrary"),
        name="rglru_sample",
    )(xr, gr, state_t, h0, *rg)


def _fox_prompt_kernel(qt_ref, kt_ref, q_ref, k_ref, vt_ref, o_ref, m_s, l_s, acc_s, *, tq, tk):
    t = pl.program_id(2)
    qi = qt_ref[t]
    ki = kt_ref[t]

    @pl.when(ki == 0)
    def _():
        m_s[...] = jnp.full_like(m_s, -jnp.inf)
        l_s[...] = jnp.zeros_like(l_s)
        acc_s[...] = jnp.zeros_like(acc_s)

    def update(masked):
        for h in range(ATT_HEADS):
            lanes = slice(h * LANES, (h + 1) * LANES)
            st = lax.dot_general(k_ref[:, lanes], q_ref[:, lanes], (((1,), (1,)), ((), ())),
                                 preferred_element_type=F32)
            if masked:
                kpos = lax.broadcasted_iota(jnp.int32, (tk, tq), 0)
                qpos = lax.broadcasted_iota(jnp.int32, (tk, tq), 1)
                st = jnp.where(kpos <= qpos, st, -jnp.inf)
            m_prev = m_s[h]
            m_new = jnp.maximum(m_prev, jnp.max(st, axis=0, keepdims=True))
            alpha = jnp.exp(m_prev - m_new)
            p = jnp.exp(st - m_new)
            l_s[h] = alpha * l_s[h] + jnp.sum(p, axis=0, keepdims=True)
            pv = jnp.dot(vt_ref[0, h * HEAD_DIM:(h + 1) * HEAD_DIM, :], p.astype(BF16),
                         preferred_element_type=F32)
            acc_s[h] = alpha * acc_s[h] + pv
            m_s[h] = m_new

    @pl.when(ki < qi)
    def _():
        update(False)

    @pl.when(ki == qi)
    def _():
        update(True)
        ot = jnp.concatenate([acc_s[h] / l_s[h] for h in range(ATT_HEADS)], axis=0)
        o_ref[...] = ot.T.astype(BF16)


def _fox_prompt(q_aug, k_aug, vt, n_batch, seq):
    tq = tk = ATT_TILE
    nq = seq // tq
    pairs = [(i, j) for i in range(nq) for j in range(i + 1)]
    qt = jnp.asarray(np.array([p[0] for p in pairs], np.int32))
    kt = jnp.asarray(np.array([p[1] for p in pairs], np.int32))
    groups = N_HEADS // ATT_HEADS
    wide = ATT_HEADS * LANES
    narrow = ATT_HEADS * HEAD_DIM
    grid_spec = pltpu.PrefetchScalarGridSpec(
        num_scalar_prefetch=2,
        grid=(n_batch, groups, len(pairs)),
        in_specs=[
            pl.BlockSpec((tq, wide), lambda b, g, t, qt, kt: (b * nq + qt[t], g)),
            pl.BlockSpec((tk, wide), lambda b, g, t, qt, kt: (b * nq + kt[t], g)),
            pl.BlockSpec((1, narrow, tk), lambda b, g, t, qt, kt: (b, g, kt[t])),
        ],
        out_specs=pl.BlockSpec((tq, narrow), lambda b, g, t, qt, kt: (b * nq + qt[t], g)),
        scratch_shapes=[pltpu.VMEM((ATT_HEADS, 1, tq), F32), pltpu.VMEM((ATT_HEADS, 1, tq), F32),
                        pltpu.VMEM((ATT_HEADS, HEAD_DIM, tq), F32)],
    )
    return pl.pallas_call(
        functools.partial(_fox_prompt_kernel, tq=tq, tk=tk),
        grid_spec=grid_spec,
        out_shape=jax.ShapeDtypeStruct((n_batch * seq, D_MODEL), BF16),
        compiler_params=_params("arbitrary", "arbitrary", "arbitrary"),
        name="fox_prompt",
    )(qt, kt, q_aug, k_aug, vt)


def _fox_paged_kernel(pt_ref, qr_ref, qc_ref, kc_ref, vc_ref, lc_ref, *rest, n_steps, g):
    k_refs = rest[0:g]
    v_refs = rest[g:2 * g]
    lf_refs = rest[2 * g:3 * g]
    o_ref = rest[3 * g]
    qb_s, m_s, l_s, acc_s, car_s = rest[3 * g + 1:]
    p = pl.program_id(1)

    @pl.when(p == 0)
    def _():
        qb_s[...] = jnp.broadcast_to(qc_ref[0].astype(F32), qb_s.shape)
        m_s[...] = jnp.full_like(m_s, -jnp.inf)
        l_s[...] = jnp.zeros_like(l_s)
        acc_s[...] = jnp.zeros_like(acc_s)
        car_s[...] = jnp.zeros_like(car_s)

    lane = lax.broadcasted_iota(jnp.int32, (N_HEADS, PAGE_SIZE), 1)

    carry = car_s[...]
    bias = []
    for j in range(g):
        pref = lf_refs[j][0, 0]
        d = 1
        while d < PAGE_SIZE:
            pref = pref + jnp.where(lane >= d, pltpu.roll(pref, d, axis=1), 0.0)
            d *= 2
        bias.append(carry + pref)
        carry = carry + pref[:, PAGE_SIZE - 1:PAGE_SIZE]
    car_s[...] = carry

    m_parts, l_parts = [], []
    for hg in range(N_HEADS // SUBLANES):
        heads = range(hg * SUBLANES, (hg + 1) * SUBLANES)
        grp = slice(hg * SUBLANES, (hg + 1) * SUBLANES)
        s = jnp.concatenate(
            [jnp.concatenate([jnp.sum(k_refs[j][0, 0, h] * qb_s[h], axis=0, keepdims=True)
                              for h in heads], axis=0) - bias[j][grp, :]
             for j in range(g)], axis=1)
        m_prev = m_s[grp, :]
        m_grp = jnp.maximum(m_prev, jnp.max(s, axis=1, keepdims=True))
        alpha = jnp.exp(m_prev - m_grp)
        pr = jnp.exp(s - m_grp)
        l_parts.append(alpha * l_s[grp, :] + jnp.sum(pr, axis=1, keepdims=True))
        m_parts.append(m_grp)
        for i, h in enumerate(heads):
            a_h = acc_s[h] * alpha[i:i + 1, :]
            for j in range(g):
                a_h = a_h + pr[i:i + 1, j * PAGE_SIZE:(j + 1) * PAGE_SIZE] * v_refs[j][0, 0, h]
            acc_s[h] = a_h
    m_new = jnp.concatenate(m_parts, axis=0)
    l_new = jnp.concatenate(l_parts, axis=0)
    l_s[...] = l_new
    m_s[...] = m_new

    @pl.when(p == n_steps - 1)
    def _():
        s_c = jnp.sum(qr_ref[0].astype(F32) * kc_ref[0], axis=1, keepdims=True) - (carry + lc_ref[0])
        m_f = jnp.maximum(m_new, s_c)
        al = jnp.exp(m_new - m_f)
        p_c = jnp.exp(s_c - m_f)
        inv_l = 1.0 / (al * l_new + p_c)
        for h in range(N_HEADS):
            past = jnp.sum(acc_s[h], axis=1, keepdims=True) * al[h:h + 1, :]
            o_ref[0, h] = (past + p_c[h:h + 1, :] * vc_ref[0, h]) * inv_l[h:h + 1, :]


def _fox_paged(q, k_cur, v_cur, lf_cur, cache_kt, cache_vt, cache_lft, page_table):
    n, n_pages = page_table.shape
    g = PAGES_PER_STEP
    n_steps = n_pages // g
    row = pl.BlockSpec((1, N_HEADS, HEAD_DIM), lambda b, p, pt: (b, 0, 0))
    col = pl.BlockSpec((1, N_HEADS, HEAD_DIM, 1), lambda b, p, pt: (b, 0, 0, 0))

    def page(tail, j):
        zeros = (0,) * len(tail)
        return pl.BlockSpec((1, 1) + tail, lambda b, p, pt: (0, pt[b, p * g + j]) + zeros)

    grid_spec = pltpu.PrefetchScalarGridSpec(
        num_scalar_prefetch=1,
        grid=(n, n_steps),
        in_specs=([row, col, row, col, pl.BlockSpec((1, N_HEADS, 1), lambda b, p, pt: (b, 0, 0))]
                  + [page((N_HEADS, HEAD_DIM, PAGE_SIZE), j) for j in range(g)]
                  + [page((N_HEADS, HEAD_DIM, PAGE_SIZE), j) for j in range(g)]
                  + [page((N_HEADS, PAGE_SIZE), j) for j in range(g)]),
        out_specs=col,
        scratch_shapes=[pltpu.VMEM((N_HEADS, HEAD_DIM, PAGE_SIZE), F32),
                        pltpu.VMEM((N_HEADS, 1), F32), pltpu.VMEM((N_HEADS, 1), F32),
                        pltpu.VMEM((N_HEADS, HEAD_DIM, PAGE_SIZE), F32),
                        pltpu.VMEM((N_HEADS, 1), F32)],
    )
    return pl.pallas_call(
        functools.partial(_fox_paged_kernel, n_steps=n_steps, g=g),
        grid_spec=grid_spec,
        out_shape=jax.ShapeDtypeStruct((n, N_HEADS, HEAD_DIM, 1), F32),
        compiler_params=_params("arbitrary", "arbitrary"),
        name="fox_paged",
    )(page_table, q, q[..., None], k_cur, v_cur[..., None], lf_cur,
      *([cache_kt] * g), *([cache_vt] * g), *([cache_lft] * g))


def _merge_kernel(x_ref, a_ref, b_ref, ga_ref, gb_ref, g1_ref, sh_ref, sc_ref, n2_ref,
                  wa_ref, wb_ref, wo_ref, x1_ref, h2_ref):
    ya = jnp.dot(a_ref[...], wa_ref[...], preferred_element_type=F32)
    yb = jnp.dot(b_ref[...], wb_ref[...], preferred_element_type=F32)
    mix = (_sigmoid(ga_ref[...]) * ya + _sigmoid(gb_ref[...]) * yb).astype(BF16)
    x1 = x_ref[...] + g1_ref[0] * jnp.dot(mix, wo_ref[...], preferred_element_type=F32)
    x1_ref[...] = x1
    h2_ref[...] = (_rms(x1, n2_ref[...]) * (1.0 + sc_ref[0]) + sh_ref[0]).astype(BF16)


def _merge(x, a_in, b_in, ga, gb, mod, rows_per_group, tm, norm_g, wa, wb, wo):
    m = x.shape[0]
    r = mod.shape[1]
    blk = pl.BlockSpec((tm, D_MODEL), lambda i: (i, 0))
    sq = _resident((D_MODEL, D_MODEL))
    return pl.pallas_call(
        _merge_kernel,
        grid=(m // tm,),
        in_specs=[blk, blk, blk, blk, blk,
                  _mod_spec(rows_per_group, tm, r, 2), _mod_spec(rows_per_group, tm, r, 3),
                  _mod_spec(rows_per_group, tm, r, 4), _resident((1, D_MODEL)), sq, sq, sq],
        out_specs=[blk, blk],
        out_shape=[jax.ShapeDtypeStruct((m, D_MODEL), F32), jax.ShapeDtypeStruct((m, D_MODEL), BF16)],
        compiler_params=_params("arbitrary"),
        name="merge_proj",
    )(x, a_in, b_in, ga, gb, mod, mod, mod, norm_g, wa, wb, wo)


def _ffn_kernel(x1_ref, h2_ref, g2_ref, nf_ref, wi_ref, wo_ref, y_ref):
    h2 = h2_ref[...]
    gu = jnp.dot(h2, wi_ref[:, 0:D_FF], preferred_element_type=F32)
    up = jnp.dot(h2, wi_ref[:, D_FF:2 * D_FF], preferred_element_type=F32)
    act = (gu * _sigmoid(gu) * up).astype(BF16)
    x2 = x1_ref[...] + g2_ref[0] * jnp.dot(act, wo_ref[...], preferred_element_type=F32)
    y_ref[...] = _rms(x2, nf_ref[...])


def _ffn(x1, h2, mod, rows_per_group, tm, normf_g, w_in, w_out):
    m = x1.shape[0]
    r = mod.shape[1]
    blk = pl.BlockSpec((tm, D_MODEL), lambda i: (i, 0))
    return pl.pallas_call(
        _ffn_kernel,
        grid=(m // tm,),
        in_specs=[blk, blk, _mod_spec(rows_per_group, tm, r, 5), _resident((1, D_MODEL)),
                  _resident((D_MODEL, 2 * D_FF)), _resident((D_FF, D_MODEL))],
        out_specs=blk,
        out_shape=jax.ShapeDtypeStruct((m, D_MODEL), F32),
        compiler_params=_params("arbitrary"),
        name="ffn_final",
    )(x1, h2, mod, normf_g, w_in, w_out)


def _pair_blocks(w):
    per = MXU_DIM // (D_MODEL // N_BLK)
    bw = D_MODEL // N_BLK
    w = w.reshape(N_BLK // per, per, bw, bw)
    eye = jnp.eye(per, dtype=w.dtype)
    return jnp.einsum("cpij,pq->cpiqj", w, eye).reshape(N_BLK // per, MXU_DIM, MXU_DIM).astype(BF16)


def kernel(x_prompt, x_sample, c_prompt, c_sample, cache_k, cache_v, cache_logf, state_conv,
           state_rglru, page_table, ada_w, ada_b, norm1_g, norm2_g, normf_g, w_in, b_f, conv_w,
           conv_b, rg_wr, rg_br, rg_wi, rg_bi, rg_lambda, w_proj_a, w_proj_b, w_o, w_ffn_in,
           w_ffn_out):
    depth = ada_w.shape[0]
    assert depth == 1, "single-layer trunk"
    bsz, seq, d = x_prompt.shape
    nsmp = x_sample.shape[0]
    assert x_sample.shape[1] == 1 and d == D_MODEL
    assert ROW_TILE == SCAN_TILE, "in_proj permutes rows per tile for the recurrence kernel"
    past_len = page_table.shape[1] * PAGE_SIZE
    attn_w = N_HEADS * HEAD_DIM
    hd = (N_HEADS, HEAD_DIM)

    wl = w_in[0]
    cut = 2 * D_MODEL + 3 * attn_w
    w_all = jnp.concatenate(
        [wl[:, :cut], wl[:, cut + N_HEADS:], jnp.pad(wl[:, cut:cut + N_HEADS], ((0, 0), (0, LANES - N_HEADS)))],
        axis=1).astype(BF16)
    bf_pad = jnp.pad(b_f[0], (0, LANES - N_HEADS)).reshape(1, LANES)
    rg = (conv_w[0], conv_b[0].reshape(1, d), _pair_blocks(rg_wr[0]), _pair_blocks(rg_wi[0]),
          rg_br[0].reshape(1, d), rg_bi[0].reshape(1, d), rg_lambda[0].reshape(1, d))
    wa, wb, wo = (w[0].astype(BF16) for w in (w_proj_a, w_proj_b, w_o))
    wfi, wfo = w_ffn_in[0].astype(BF16), w_ffn_out[0].astype(BF16)
    n1, n2, nf = norm1_g[0].reshape(1, d), norm2_g[0].reshape(1, d), normf_g.reshape(1, d)

    mod = _ada(jnp.concatenate([c_prompt, c_sample], axis=0), ada_w[0], ada_b[0])
    mod_p = mod[:bsz].reshape(bsz, 1, 6 * d)
    mod_s = mod[bsz:].reshape(1, nsmp, 6 * d)

    xp = x_prompt.reshape(bsz * seq, d)
    xr, gr, ga, gb, lf, q_aug, k_aug, kt, vt, vtb = _in_proj(xp, mod_p, seq, ROW_TILE, n1, w_all,
                                                             bf_pad, True)
    logf_p = lf[:, :N_HEADS].reshape(bsz, seq, N_HEADS)
    a_in, h_last_p = _rglru_seq(xr, gr, bsz, seq, rg)
    att = _fox_prompt(q_aug, k_aug, vtb, bsz, seq)
    k_p = jnp.transpose(kt.reshape(1, bsz, *hd, seq), (0, 1, 4, 2, 3))
    v_p = jnp.transpose(vt.reshape(1, bsz, *hd, seq), (0, 1, 4, 2, 3))
    x1, h2 = _merge(xp, a_in, att, ga, gb, mod_p, seq, ROW_TILE, n2, wa, wb, wo)
    y_p = _ffn(x1, h2, mod_p, seq, ROW_TILE, nf, wfi, wfo).reshape(bsz, seq, d)
    last = [seq - ROW_TILE + (ROW_TILE // SUBLANES - (CONV_W - 1) + i) * SUBLANES + SUBLANES - 1
            for i in range(CONV_W - 1)]
    xr3 = xr.reshape(bsz, seq, d)
    conv_p = jnp.stack([xr3[:, r] for r in last], axis=1)

    xs = x_sample.reshape(nsmp, d)
    xr_s, gr_s, ga_s, gb_s, lf_s, q_s, k_s, v_s = _in_proj(xs, mod_s, nsmp, nsmp, n1, w_all,
                                                           bf_pad, False)
    logf_s = lf_s[:, :N_HEADS]
    a_in_s, h_s = _rglru_step(xr_s, gr_s, jnp.transpose(state_conv[0], (1, 0, 2)), state_rglru[0],
                              rg, past_len == 0)
    att_s = _fox_paged(q_s.astype(F32).reshape(nsmp, *hd), k_s.reshape(nsmp, *hd),
                       v_s.reshape(nsmp, *hd), logf_s.reshape(nsmp, N_HEADS, 1),
                       jnp.transpose(cache_k, (0, 1, 3, 4, 2)), jnp.transpose(cache_v, (0, 1, 3, 4, 2)),
                       jnp.transpose(cache_logf, (0, 1, 3, 2)), page_table)
    x1_s, h2_s = _merge(xs, a_in_s, att_s.reshape(nsmp, d).astype(BF16), ga_s, gb_s, mod_s, nsmp,
                        nsmp, n2, wa, wb, wo)
    y_s = _ffn(x1_s, h2_s, mod_s, nsmp, nsmp, nf, wfi, wfo).reshape(nsmp, 1, d)
    conv_s = jnp.concatenate([state_conv[0][:, 1:], xr_s[:, None, :]], axis=1)

    return (y_p, y_s,
            k_p, v_p, logf_p[None],
            conv_p[None], h_last_p.reshape(1, bsz, d),
            k_s.reshape(1, nsmp, 1, *hd), v_s.reshape(1, nsmp, 1, *hd),
            logf_s.reshape(1, nsmp, 1, N_HEADS),
            conv_s[None], h_s[None])
```

```python
import functools

import numpy as np
import jax
import jax.numpy as jnp
from jax import lax
from jax.experimental import pallas as pl
from jax.experimental.pallas import tpu as pltpu

F32 = jnp.float32
BF16 = jnp.bfloat16

D_MODEL = 1024
N_HEADS = 16
HEAD_DIM = 64
N_BLK = 16
CONV_W = 4
RG_C = 8.0
PAGE_SIZE = 128
D_FF = 2816
EPS = 1e-6

LANES = 128
SUBLANES = 8
MXU_DIM = 256
VMEM_LIMIT = 56 * 1024 * 1024

ROW_TILE = 256
SCAN_TILE = 256
ATT_TILE = 512
ATT_HEADS = 8
PAGES_PER_STEP = 8

N_SEG = 7
W_ALL_COLS = N_SEG * D_MODEL + LANES

AUG = HEAD_DIM
N_SPLIT = 3
ONES_LANE = N_SPLIT * N_HEADS


def _params(*sem):
    return pltpu.CompilerParams(dimension_semantics=sem, vmem_limit_bytes=VMEM_LIMIT)


def _resident(shape):
    nd = len(shape)
    return pl.BlockSpec(shape, lambda *_: (0,) * nd, pipeline_mode=pl.Buffered(1))


def _sigmoid(x):
    return 1.0 / (1.0 + jnp.exp(-x))


def _log_sigmoid(x):
    return jnp.minimum(x, 0.0) - jnp.log1p(jnp.exp(-jnp.abs(x)))


def _softplus(x):
    return jnp.maximum(x, 0.0) + jnp.log1p(jnp.exp(-jnp.abs(x)))


def _rms(x, g):
    return x * lax.rsqrt(jnp.mean(x * x, axis=-1, keepdims=True) + EPS) * g


def _ada_kernel(c_ref, w_ref, b_ref, o_ref):
    c = c_ref[...]
    s = (c * _sigmoid(c)).astype(BF16)
    o_ref[...] = jnp.dot(s, w_ref[...].astype(BF16), preferred_element_type=F32) + b_ref[...]


def _ada(c, w, b):
    n, d = c.shape
    cols = w.shape[1]
    return pl.pallas_call(
        _ada_kernel,
        grid=(cols // d,),
        in_specs=[pl.BlockSpec((n, d), lambda j: (0, 0)),
                  pl.BlockSpec((d, d), lambda j: (0, j)),
                  pl.BlockSpec((1, d), lambda j: (0, j))],
        out_specs=pl.BlockSpec((n, d), lambda j: (0, j)),
        out_shape=jax.ShapeDtypeStruct((n, cols), F32),
        compiler_params=_params("arbitrary"),
        name="ada_modulation",
    )(c, w, b.reshape(1, cols))


def _mod_spec(rows_per_group, tm, r, chunk):
    tiles = rows_per_group // tm
    return pl.BlockSpec((1, r, D_MODEL), lambda i: (i // tiles, 0, chunk))


def _split3(x):
    hi = x.astype(BF16).astype(F32)
    r = x - hi
    mid = r.astype(BF16).astype(F32)
    return hi, mid, r - mid


def _in_proj_kernel(*refs, prompt, tiles_per_seq):
    if prompt:
        (x_ref, g_ref, sh_ref, sc_ref, w_ref, bf_ref, tri_ref, pq_ref, pk_ref, perm_ref,
         xr_ref, gr_ref, ga_ref, gb_ref, lf_ref, qa_ref, ka_ref, kt_ref, vt_ref, vtb_ref, fcar) = refs
    else:
        (x_ref, g_ref, sh_ref, sc_ref, w_ref, bf_ref,
         xr_ref, gr_ref, ga_ref, gb_ref, lf_ref, q_ref, k_ref, v_ref) = refs
    x = x_ref[...]
    h = (_rms(x, g_ref[...]) * (1.0 + sc_ref[0]) + sh_ref[0]).astype(BF16)

    def seg(s, lhs=h):
        return jnp.dot(lhs, w_ref[:, s * D_MODEL:(s + 1) * D_MODEL], preferred_element_type=F32)

    if prompt:
        hs = jnp.dot(perm_ref[...], h, preferred_element_type=F32).astype(BF16)
        xr_ref[...] = seg(0, hs)
        gr_ref[...] = seg(1, hs)
    else:
        xr_ref[...] = seg(0)
        gr_ref[...] = seg(1)
    q = (seg(2) * (HEAD_DIM ** -0.5)).astype(BF16)
    k = seg(3)
    v = seg(4)
    ga_ref[...] = seg(5)
    gb_ref[...] = seg(6)
    fl = jnp.dot(h, w_ref[:, N_SEG * D_MODEL:], preferred_element_type=F32)
    lf = _log_sigmoid(fl + bf_ref[...])
    lf_ref[...] = lf
    if not prompt:
        q_ref[...] = q
        k_ref[...] = k
        v_ref[...] = v
        return

    tm = x.shape[0]
    kt_ref[0] = k.T
    vt = v.T
    vt_ref[0] = vt
    vtb_ref[0] = vt.astype(BF16)

    @pl.when(pl.program_id(0) % tiles_per_seq == 0)
    def _():
        fcar[...] = jnp.zeros_like(fcar)

    lane = lax.broadcasted_iota(jnp.int32, (1, LANES), 1)
    tri = tri_ref[...]
    cum = fcar[...]
    for piece in _split3(jnp.where(lane < N_HEADS, lf, 0.0)):
        cum = cum + jnp.dot(tri, piece.astype(BF16), preferred_element_type=F32)
    fcar[...] = cum[tm - 1:tm, :]

    c_hi, c_mid, c_lo = _split3(cum)
    faug = jnp.where(lane < N_HEADS, c_hi,
                     jnp.where(lane < 2 * N_HEADS, pltpu.roll(c_mid, N_HEADS, axis=1),
                               jnp.where(lane < ONES_LANE, pltpu.roll(c_lo, 2 * N_HEADS, axis=1),
                                         jnp.where(lane == ONES_LANE, 1.0, 0.0)))).astype(BF16)
    kb = k.astype(BF16)
    for j in range(N_HEADS // 2):
        pair = slice(j * LANES, (j + 1) * LANES)
        out = slice(j * MXU_DIM, (j + 1) * MXU_DIM)
        qa_ref[:, out] = jnp.dot(jnp.concatenate([q[:, pair], faug], axis=1), pq_ref[j],
                                 preferred_element_type=F32).astype(BF16)
        ka_ref[:, out] = jnp.dot(jnp.concatenate([kb[:, pair], faug], axis=1), pk_ref[j],
                                 preferred_element_type=F32).astype(BF16)


def _scan_order(tile):
    seg_len = tile // SUBLANES
    n = np.arange(tile)
    perm = np.zeros((tile, tile), np.float32)
    perm[n, (n % SUBLANES) * seg_len + n // SUBLANES] = 1.0
    return perm


def _placement_matrices():
    n_pairs = N_HEADS // 2
    pq = np.zeros((n_pairs, MXU_DIM, MXU_DIM), np.float32)
    pk = np.zeros((n_pairs, MXU_DIM, MXU_DIM), np.float32)
    for j in range(n_pairs):
        for s in range(2):
            head = 2 * j + s
            for d in range(HEAD_DIM):
                pq[j, s * HEAD_DIM + d, s * LANES + d] = 1.0
                pk[j, s * HEAD_DIM + d, s * LANES + d] = 1.0
            cb = s * LANES + AUG
            for i in range(N_SPLIT):
                pq[j, LANES + i * N_HEADS + head, cb + i] = 1.0
                pq[j, LANES + ONES_LANE, cb + N_SPLIT + i] = 1.0
                pk[j, LANES + ONES_LANE, cb + i] = 1.0
                pk[j, LANES + i * N_HEADS + head, cb + N_SPLIT + i] = -1.0
    return jnp.asarray(pq, BF16), jnp.asarray(pk, BF16)


def _in_proj(x, mod, rows_per_group, tm, norm_g, w_all, bf_pad, prompt):
    m = x.shape[0]
    r = mod.shape[1]
    row = lambda i: (i, 0)
    f32o = jax.ShapeDtypeStruct((m, D_MODEL), F32)
    blk = pl.BlockSpec((tm, D_MODEL), row)
    in_specs = [blk, _resident((1, D_MODEL)),
                _mod_spec(rows_per_group, tm, r, 0), _mod_spec(rows_per_group, tm, r, 1),
                _resident((D_MODEL, W_ALL_COLS)), _resident((1, LANES))]
    operands = [x, norm_g, mod, mod, w_all, bf_pad]
    out_specs = [blk] * 4 + [pl.BlockSpec((tm, LANES), row)]
    out_shape = [f32o] * 4 + [jax.ShapeDtypeStruct((m, LANES), F32)]
    scratch = []
    if prompt:
        tiles = rows_per_group // tm
        n_seq = m // rows_per_group
        tri = jnp.asarray(np.tril(np.ones((tm, tm), np.float32)), BF16)
        pq, pk = _placement_matrices()
        perm = jnp.asarray(_scan_order(tm), BF16)
        in_specs += [_resident(tri.shape), _resident(pq.shape), _resident(pk.shape),
                     _resident(perm.shape)]
        operands += [tri, pq, pk, perm]
        wide = pl.BlockSpec((tm, 2 * D_MODEL), row)
        time_on_lanes = pl.BlockSpec((1, D_MODEL, tm), lambda i: (i // tiles, 0, i % tiles))
        out_specs += [wide, wide] + [time_on_lanes] * 3
        out_shape += [jax.ShapeDtypeStruct((m, 2 * D_MODEL), BF16)] * 2
        out_shape += [jax.ShapeDtypeStruct((n_seq, D_MODEL, rows_per_group), F32)] * 2
        out_shape += [jax.ShapeDtypeStruct((n_seq, D_MODEL, rows_per_group), BF16)]
        scratch = [pltpu.VMEM((1, LANES), F32)]
    else:
        out_specs += [blk] * 3
        out_shape += [jax.ShapeDtypeStruct((m, D_MODEL), BF16), f32o, f32o]
    return pl.pallas_call(
        functools.partial(_in_proj_kernel, prompt=prompt, tiles_per_seq=rows_per_group // tm),
        grid=(m // tm,),
        in_specs=in_specs, out_specs=out_specs, out_shape=out_shape, scratch_shapes=scratch,
        compiler_params=_params("arbitrary"),
        name="in_proj_prompt" if prompt else "in_proj_sample",
    )(*operands)


def _block_gates(xc, wr_ref, wi_ref, br, bi):
    xcb = xc.astype(BF16)
    n = D_MODEL // MXU_DIM
    pr = [jnp.dot(xcb[:, c * MXU_DIM:(c + 1) * MXU_DIM], wr_ref[c], preferred_element_type=F32)
          for c in range(n)]
    pi = [jnp.dot(xcb[:, c * MXU_DIM:(c + 1) * MXU_DIM], wi_ref[c], preferred_element_type=F32)
          for c in range(n)]
    gate_r = _sigmoid(jnp.concatenate(pr, axis=1) + br)
    gate_i = _sigmoid(jnp.concatenate(pi, axis=1) + bi)
    return gate_r, gate_i


def _decay_and_input(xc, gate_r, gate_i, lam, is_start):
    log_a = -RG_C * gate_r * _softplus(-lam)
    a = jnp.exp(log_a)
    mult = jnp.sqrt(jnp.tanh(-log_a) * (a * a + 1.0))
    if is_start is not None:
        mult = jnp.where(is_start, 1.0, mult)
    return a, mult * gate_i * xc


def _rglru_seq_kernel(xr_ref, gr_ref, unperm_ref, cw_ref, cb_ref, wr_ref, wi_ref, br_ref, bi_ref,
                      lam_ref, o_ref, hl_ref, xbuf, prev_s, hcar, *, tt):
    t = pl.program_id(1)
    seg_len = tt // SUBLANES
    halo = (CONV_W - 1) * SUBLANES

    @pl.when(t == 0)
    def _():
        prev_s[...] = jnp.zeros_like(prev_s)
        hcar[...] = jnp.zeros_like(hcar)

    x = xr_ref[...]
    sub = lax.broadcasted_iota(jnp.int32, (SUBLANES, 1), 0)
    for i in range(CONV_W - 1):
        rows = slice(i * SUBLANES, (i + 1) * SUBLANES)
        cur = x[tt - halo + i * SUBLANES:tt - halo + (i + 1) * SUBLANES, :]
        xbuf[rows, :] = jnp.where(sub == 0, pltpu.roll(prev_s[rows, :], 1, axis=0),
                                  pltpu.roll(cur, 1, axis=0))
    prev_s[...] = x[tt - halo:, :]
    xbuf[halo:, :] = x
    xc = cb_ref[...] + cw_ref[CONV_W - 1:CONV_W, :] * x
    for j in range(CONV_W - 1):
        back = CONV_W - 1 - j
        xc = xc + cw_ref[j:j + 1, :] * xbuf[pl.ds(halo - back * SUBLANES, tt), :]

    gate_r, gate_i = _block_gates(xc, wr_ref, wi_ref, br_ref[...], bi_ref[...])
    is_start = (lax.broadcasted_iota(jnp.int32, (tt, 1), 0) + t * tt) == 0
    a, u = _decay_and_input(xc, gate_r, gate_i, lam_ref[...], is_start)

    hh = jnp.zeros((SUBLANES, D_MODEL), F32)
    pp = jnp.ones((SUBLANES, D_MODEL), F32)
    local, decay = [], []
    for g in range(seg_len):
        rows = slice(g * SUBLANES, (g + 1) * SUBLANES)
        hh = a[rows, :] * hh + u[rows, :]
        pp = a[rows, :] * pp
        local.append(hh)
        decay.append(pp)
    cy = hcar[0:1, :]
    carries = []
    for s in range(SUBLANES):
        carries.append(cy)
        cy = hh[s:s + 1, :] + pp[s:s + 1, :] * cy
    hcar[0:1, :] = cy
    hl_ref[0] = cy
    cin = jnp.concatenate(carries, axis=0)
    hr = jnp.concatenate([local[g] + decay[g] * cin for g in range(seg_len)], axis=0)
    gated = (hr * jax.nn.gelu(gr_ref[...])).astype(BF16)
    o_ref[...] = jnp.dot(unperm_ref[...], gated, preferred_element_type=F32).astype(BF16)


def _rglru_seq(xr, gr, n_batch, seq, rg):
    tt = SCAN_TILE
    nt = seq // tt
    blk = pl.BlockSpec((tt, D_MODEL), lambda b, t: (b * nt + t, 0))
    nchunk = D_MODEL // MXU_DIM
    unperm = jnp.asarray(_scan_order(tt).T, BF16)
    return pl.pallas_call(
        functools.partial(_rglru_seq_kernel, tt=tt),
        grid=(n_batch, nt),
        in_specs=[blk, blk, _resident((tt, tt)), _resident((CONV_W, D_MODEL)),
                  _resident((1, D_MODEL)),
                  _resident((nchunk, MXU_DIM, MXU_DIM)), _resident((nchunk, MXU_DIM, MXU_DIM)),
                  _resident((1, D_MODEL)), _resident((1, D_MODEL)), _resident((1, D_MODEL))],
        out_specs=[blk, pl.BlockSpec((1, 1, D_MODEL), lambda b, t: (b, 0, 0))],
        out_shape=[jax.ShapeDtypeStruct((n_batch * seq, D_MODEL), BF16),
                   jax.ShapeDtypeStruct((n_batch, 1, D_MODEL), F32)],
        scratch_shapes=[pltpu.VMEM((tt + (CONV_W - 1) * SUBLANES, D_MODEL), F32),
                        pltpu.VMEM(((CONV_W - 1) * SUBLANES, D_MODEL), F32),
                        pltpu.VMEM((SUBLANES, D_MODEL), F32)],
        compiler_params=_params("arbitrary", "arbitrary"),
        name="rglru_prompt",
    )(xr, gr, unperm, *rg)


def _rglru_step_kernel(xr_ref, gr_ref, st_ref, h0_ref, cw_ref, cb_ref, wr_ref, wi_ref, br_ref,
                       bi_ref, lam_ref, o_ref, h_ref, *, at_start):
    xr = xr_ref[...]
    xc = cb_ref[...] + cw_ref[CONV_W - 1:CONV_W, :] * xr
    for j in range(CONV_W - 1):
        xc = xc + cw_ref[j:j + 1, :] * st_ref[j]
    gate_r, gate_i = _block_gates(xc, wr_ref, wi_ref, br_ref[...], bi_ref[...])
    a, u = _decay_and_input(xc, gate_r, gate_i, lam_ref[...], True if at_start else None)
    h = a * h0_ref[...] + u
    h_ref[...] = h
    o_ref[...] = (h * jax.nn.gelu(gr_ref[...])).astype(BF16)


def _rglru_step(xr, gr, state_t, h0, rg, at_start):
    n = xr.shape[0]
    full = lambda shape: pl.BlockSpec(shape, lambda i: (0,) * len(shape))
    nchunk = D_MODEL // MXU_DIM
    return pl.pallas_call(
        functools.partial(_rglru_step_kernel, at_start=at_start),
        grid=(1,),
        in_specs=[full((n, D_MODEL)), full((n, D_MODEL)), full((CONV_W - 1, n, D_MODEL)),
                  full((n, D_MODEL)), full((CONV_W, D_MODEL)), full((1, D_MODEL)),
                  full((nchunk, MXU_DIM, MXU_DIM)), full((nchunk, MXU_DIM, MXU_DIM)),
                  full((1, D_MODEL)), full((1, D_MODEL)), full((1, D_MODEL))],
        out_specs=[full((n, D_MODEL)), full((n, D_MODEL))],
        out_shape=[jax.ShapeDtypeStruct((n, D_MODEL), BF16),
                   jax.ShapeDtypeStruct((n, D_MODEL), F32)],
        compiler_params=_params("arbitrary"),
        name="rglru_sample",
    )(xr, gr, state_t, h0, *rg)


def _prompt_update(q_ref, k_ref, vt_ref, m_s, l_s, acc_s, masked, tq, tk):
    for h in range(ATT_HEADS):
        lanes = slice(h * LANES, (h + 1) * LANES)
        st = lax.dot_general(k_ref[:, lanes], q_ref[:, lanes], (((1,), (1,)), ((), ())),
                             preferred_element_type=F32)
        if masked:
            kpos = lax.broadcasted_iota(jnp.int32, (tk, tq), 0)
            qpos = lax.broadcasted_iota(jnp.int32, (tk, tq), 1)
            st = jnp.where(kpos <= qpos, st, -jnp.inf)
        m_prev = m_s[h]
        m_new = jnp.maximum(m_prev, jnp.max(st, axis=0, keepdims=True))
        alpha = jnp.exp(m_prev - m_new)
        p = jnp.exp(st - m_new)
        l_s[h] = alpha * l_s[h] + jnp.sum(p, axis=0, keepdims=True)
        pv = jnp.dot(vt_ref[0, h * HEAD_DIM:(h + 1) * HEAD_DIM, :], p.astype(BF16),
                     preferred_element_type=F32)
        acc_s[h] = alpha * acc_s[h] + pv
        m_s[h] = m_new


def _paged_update(valid, k_refs, v_refs, lf_refs, qb_s, m_s, l_s, acc_s, car_s):
    g = len(k_refs)
    lane = lax.broadcasted_iota(jnp.int32, (N_HEADS, PAGE_SIZE), 1)

    carry_in = car_s[...]
    carry = carry_in
    bias = []
    for j in range(g):
        pref = lf_refs[j][0, 0]
        d = 1
        while d < PAGE_SIZE:
            pref = pref + jnp.where(lane >= d, pltpu.roll(pref, d, axis=1), 0.0)
            d *= 2
        bias.append(carry + pref)
        carry = carry + pref[:, PAGE_SIZE - 1:PAGE_SIZE]
    car_s[...] = jnp.where(valid, carry, carry_in)

    for hg in range(N_HEADS // SUBLANES):
        heads = range(hg * SUBLANES, (hg + 1) * SUBLANES)
        grp = slice(hg * SUBLANES, (hg + 1) * SUBLANES)
        s = jnp.concatenate(
            [jnp.concatenate([jnp.sum(k_refs[j][0, 0, h] * qb_s[h], axis=0, keepdims=True)
                              for h in heads], axis=0) - bias[j][grp, :]
             for j in range(g)], axis=1)
        m_prev = m_s[grp, :]
        m_grp = jnp.where(valid, jnp.maximum(m_prev, jnp.max(s, axis=1, keepdims=True)), m_prev)
        alpha = jnp.where(valid, jnp.exp(m_prev - m_grp), 1.0)
        pr = jnp.where(valid, jnp.exp(s - m_grp), 0.0)
        l_s[grp, :] = alpha * l_s[grp, :] + jnp.sum(pr, axis=1, keepdims=True)
        m_s[grp, :] = m_grp
        for i, h in enumerate(heads):
            a_h = acc_s[h] * alpha[i:i + 1, :]
            for j in range(g):
                a_h = a_h + pr[i:i + 1, j * PAGE_SIZE:(j + 1) * PAGE_SIZE] * v_refs[j][0, 0, h]
            acc_s[h] = a_h


def _paged_finish(qr_ref, kc_ref, vc_ref, lc_ref, o_ref, m_s, l_s, acc_s, car_s):
    m_past = m_s[...]
    s_c = jnp.sum(qr_ref[0] * kc_ref[0], axis=1, keepdims=True) - (car_s[...] + lc_ref[0])
    m_f = jnp.maximum(m_past, s_c)
    al = jnp.exp(m_past - m_f)
    p_c = jnp.exp(s_c - m_f)
    inv_l = 1.0 / (al * l_s[...] + p_c)
    for h in range(N_HEADS):
        past = jnp.sum(acc_s[h], axis=1, keepdims=True) * al[h:h + 1, :]
        o_ref[0, h] = (past + p_c[h:h + 1, :] * vc_ref[0, h]) * inv_l[h:h + 1, :]


def _fox_kernel(qt_ref, kt_ref, pt_ref, q_ref, k_ref, vt_ref, qr_ref, qc_ref, kc_ref, vc_ref,
                lc_ref, *rest, tq, tk, g, n_paged_steps, steps_per_seq):
    k_refs = rest[0:g]
    v_refs = rest[g:2 * g]
    lf_refs = rest[2 * g:3 * g]
    o_ref, os_ref = rest[3 * g], rest[3 * g + 1]
    m_s, l_s, acc_s, qb_s, pm_s, pl_s, pacc_s, car_s = rest[3 * g + 2:]
    t = pl.program_id(2)
    qi = qt_ref[t]
    ki = kt_ref[t]
    step = (pl.program_id(0) * pl.num_programs(1) + pl.program_id(1)) * pl.num_programs(2) + t
    valid = step < n_paged_steps
    page_step = jnp.minimum(step, n_paged_steps - 1) % steps_per_seq

    @pl.when(ki == 0)
    def _():
        m_s[...] = jnp.full_like(m_s, -jnp.inf)
        l_s[...] = jnp.zeros_like(l_s)
        acc_s[...] = jnp.zeros_like(acc_s)

    @pl.when(jnp.logical_and(valid, page_step == 0))
    def _():
        qb_s[...] = jnp.broadcast_to(qc_ref[0], qb_s.shape)
        pm_s[...] = jnp.full_like(pm_s, -jnp.inf)
        pl_s[...] = jnp.zeros_like(pl_s)
        pacc_s[...] = jnp.zeros_like(pacc_s)
        car_s[...] = jnp.zeros_like(car_s)

    def both(masked):
        _prompt_update(q_ref, k_ref, vt_ref, m_s, l_s, acc_s, masked, tq, tk)
        _paged_update(valid, k_refs, v_refs, lf_refs, qb_s, pm_s, pl_s, pacc_s, car_s)

    @pl.when(ki < qi)
    def _():
        both(False)

    @pl.when(ki == qi)
    def _():
        both(True)
        ot = jnp.concatenate([acc_s[h] / l_s[h] for h in range(ATT_HEADS)], axis=0)
        o_ref[...] = ot.T.astype(BF16)

    @pl.when(jnp.logical_and(valid, page_step == steps_per_seq - 1))
    def _():
        _paged_finish(qr_ref, kc_ref, vc_ref, lc_ref, os_ref, pm_s, pl_s, pacc_s, car_s)


def _fox_attention(q_aug, k_aug, vt, n_batch, seq, q, k_cur, v_cur, lf_cur, cache_kt, cache_vt,
                   cache_lft, page_table):
    tq = tk = ATT_TILE
    nq = seq // tq
    pairs = [(i, j) for i in range(nq) for j in range(i + 1)]
    qt = jnp.asarray(np.array([p[0] for p in pairs], np.int32))
    kt = jnp.asarray(np.array([p[1] for p in pairs], np.int32))
    groups = N_HEADS // ATT_HEADS
    wide = ATT_HEADS * LANES
    narrow = ATT_HEADS * HEAD_DIM

    n, n_pages = page_table.shape
    g = PAGES_PER_STEP
    steps_per_seq = n_pages // g
    n_paged_steps = n * steps_per_seq
    assert n_paged_steps <= n_batch * groups * len(pairs), "not enough grid steps for the cache pages"

    def seq_and_step(b, gi, t):
        s = jnp.minimum((b * groups + gi) * len(pairs) + t, n_paged_steps - 1)
        return s // steps_per_seq, s % steps_per_seq

    def per_seq(shape):
        zeros = (0,) * len(shape)
        return pl.BlockSpec((1,) + shape,
                            lambda b, gi, t, qt, kt, pt: (seq_and_step(b, gi, t)[0],) + zeros)

    def page(tail, j):
        zeros = (0,) * len(tail)

        def index(b, gi, t, qt, kt, pt):
            sq, st = seq_and_step(b, gi, t)
            return (0, pt[sq, st * g + j]) + zeros
        return pl.BlockSpec((1, 1) + tail, index)

    row, col = per_seq((N_HEADS, HEAD_DIM)), per_seq((N_HEADS, HEAD_DIM, 1))
    grid_spec = pltpu.PrefetchScalarGridSpec(
        num_scalar_prefetch=3,
        grid=(n_batch, groups, len(pairs)),
        in_specs=([pl.BlockSpec((tq, wide), lambda b, gi, t, qt, kt, pt: (b * nq + qt[t], gi)),
                   pl.BlockSpec((tk, wide), lambda b, gi, t, qt, kt, pt: (b * nq + kt[t], gi)),
                   pl.BlockSpec((1, narrow, tk), lambda b, gi, t, qt, kt, pt: (b, gi, kt[t])),
                   row, col, row, col, per_seq((N_HEADS, 1))]
                  + [page((N_HEADS, HEAD_DIM, PAGE_SIZE), j) for j in range(g)]
                  + [page((N_HEADS, HEAD_DIM, PAGE_SIZE), j) for j in range(g)]
                  + [page((N_HEADS, PAGE_SIZE), j) for j in range(g)]),
        out_specs=[pl.BlockSpec((tq, narrow), lambda b, gi, t, qt, kt, pt: (b * nq + qt[t], gi)),
                   col],
        scratch_shapes=[pltpu.VMEM((ATT_HEADS, 1, tq), F32), pltpu.VMEM((ATT_HEADS, 1, tq), F32),
                        pltpu.VMEM((ATT_HEADS, HEAD_DIM, tq), F32),
                        pltpu.VMEM((N_HEADS, HEAD_DIM, PAGE_SIZE), F32),
                        pltpu.VMEM((N_HEADS, 1), F32), pltpu.VMEM((N_HEADS, 1), F32),
                        pltpu.VMEM((N_HEADS, HEAD_DIM, PAGE_SIZE), F32),
                        pltpu.VMEM((N_HEADS, 1), F32)],
    )
    return pl.pallas_call(
        functools.partial(_fox_kernel, tq=tq, tk=tk, g=g, n_paged_steps=n_paged_steps,
                          steps_per_seq=steps_per_seq),
        grid_spec=grid_spec,
        out_shape=[jax.ShapeDtypeStruct((n_batch * seq, D_MODEL), BF16),
                   jax.ShapeDtypeStruct((n, N_HEADS, HEAD_DIM, 1), F32)],
        compiler_params=_params("arbitrary", "arbitrary", "arbitrary"),
        name="fox_attention",
    )(qt, kt, page_table, q_aug, k_aug, vt, q, q[..., None], k_cur, v_cur[..., None], lf_cur,
      *([cache_kt] * g), *([cache_vt] * g), *([cache_lft] * g))


def _merge_kernel(x_ref, a_ref, b_ref, ga_ref, gb_ref, g1_ref, sh_ref, sc_ref, n2_ref,
                  wa_ref, wb_ref, wo_ref, x1_ref, h2_ref):
    ya = jnp.dot(a_ref[...], wa_ref[...], preferred_element_type=F32)
    yb = jnp.dot(b_ref[...], wb_ref[...], preferred_element_type=F32)
    mix = (_sigmoid(ga_ref[...]) * ya + _sigmoid(gb_ref[...]) * yb).astype(BF16)
    x1 = x_ref[...] + g1_ref[0] * jnp.dot(mix, wo_ref[...], preferred_element_type=F32)
    x1_ref[...] = x1
    h2_ref[...] = (_rms(x1, n2_ref[...]) * (1.0 + sc_ref[0]) + sh_ref[0]).astype(BF16)


def _merge(x, a_in, b_in, ga, gb, mod, rows_per_group, tm, norm_g, wa, wb, wo):
    m = x.shape[0]
    r = mod.shape[1]
    blk = pl.BlockSpec((tm, D_MODEL), lambda i: (i, 0))
    sq = _resident((D_MODEL, D_MODEL))
    return pl.pallas_call(
        _merge_kernel,
        grid=(m // tm,),
        in_specs=[blk, blk, blk, blk, blk,
                  _mod_spec(rows_per_group, tm, r, 2), _mod_spec(rows_per_group, tm, r, 3),
                  _mod_spec(rows_per_group, tm, r, 4), _resident((1, D_MODEL)), sq, sq, sq],
        out_specs=[blk, blk],
        out_shape=[jax.ShapeDtypeStruct((m, D_MODEL), F32), jax.ShapeDtypeStruct((m, D_MODEL), BF16)],
        compiler_params=_params("arbitrary"),
        name="merge_proj",
    )(x, a_in, b_in, ga, gb, mod, mod, mod, norm_g, wa, wb, wo)


def _ffn_kernel(x1_ref, h2_ref, g2_ref, nf_ref, wi_ref, wo_ref, y_ref):
    h2 = h2_ref[...]
    gu = jnp.dot(h2, wi_ref[:, 0:D_FF], preferred_element_type=F32)
    up = jnp.dot(h2, wi_ref[:, D_FF:2 * D_FF], preferred_element_type=F32)
    act = (gu * _sigmoid(gu) * up).astype(BF16)
    x2 = x1_ref[...] + g2_ref[0] * jnp.dot(act, wo_ref[...], preferred_element_type=F32)
    y_ref[...] = _rms(x2, nf_ref[...])


def _ffn(x1, h2, mod, rows_per_group, tm, normf_g, w_in, w_out):
    m = x1.shape[0]
    r = mod.shape[1]
    blk = pl.BlockSpec((tm, D_MODEL), lambda i: (i, 0))
    return pl.pallas_call(
        _ffn_kernel,
        grid=(m // tm,),
        in_specs=[blk, blk, _mod_spec(rows_per_group, tm, r, 5), _resident((1, D_MODEL)),
                  _resident((D_MODEL, 2 * D_FF)), _resident((D_FF, D_MODEL))],
        out_specs=blk,
        out_shape=jax.ShapeDtypeStruct((m, D_MODEL), F32),
        compiler_params=_params("arbitrary"),
        name="ffn_final",
    )(x1, h2, mod, normf_g, w_in, w_out)


def _pair_blocks(w):
    per = MXU_DIM // (D_MODEL // N_BLK)
    bw = D_MODEL // N_BLK
    w = w.reshape(N_BLK // per, per, bw, bw)
    eye = jnp.eye(per, dtype=w.dtype)
    return jnp.einsum("cpij,pq->cpiqj", w, eye).reshape(N_BLK // per, MXU_DIM, MXU_DIM).astype(BF16)


def kernel(x_prompt, x_sample, c_prompt, c_sample, cache_k, cache_v, cache_logf, state_conv,
           state_rglru, page_table, ada_w, ada_b, norm1_g, norm2_g, normf_g, w_in, b_f, conv_w,
           conv_b, rg_wr, rg_br, rg_wi, rg_bi, rg_lambda, w_proj_a, w_proj_b, w_o, w_ffn_in,
           w_ffn_out):
    depth = ada_w.shape[0]
    assert depth == 1, "single-layer trunk"
    bsz, seq, d = x_prompt.shape
    nsmp = x_sample.shape[0]
    assert x_sample.shape[1] == 1 and d == D_MODEL
    assert ROW_TILE == SCAN_TILE, "in_proj permutes rows per tile for the recurrence kernel"
    past_len = page_table.shape[1] * PAGE_SIZE
    attn_w = N_HEADS * HEAD_DIM
    hd = (N_HEADS, HEAD_DIM)

    wl = w_in[0]
    cut = 2 * D_MODEL + 3 * attn_w
    w_all = jnp.concatenate(
        [wl[:, :cut], wl[:, cut + N_HEADS:], jnp.pad(wl[:, cut:cut + N_HEADS], ((0, 0), (0, LANES - N_HEADS)))],
        axis=1).astype(BF16)
    bf_pad = jnp.pad(b_f[0], (0, LANES - N_HEADS)).reshape(1, LANES)
    rg = (conv_w[0], conv_b[0].reshape(1, d), _pair_blocks(rg_wr[0]), _pair_blocks(rg_wi[0]),
          rg_br[0].reshape(1, d), rg_bi[0].reshape(1, d), rg_lambda[0].reshape(1, d))
    wa, wb, wo = (w[0].astype(BF16) for w in (w_proj_a, w_proj_b, w_o))
    wfi, wfo = w_ffn_in[0].astype(BF16), w_ffn_out[0].astype(BF16)
    n1, n2, nf = norm1_g[0].reshape(1, d), norm2_g[0].reshape(1, d), normf_g.reshape(1, d)

    mod = _ada(jnp.concatenate([c_prompt, c_sample], axis=0), ada_w[0], ada_b[0])
    mod_p = mod[:bsz].reshape(bsz, 1, 6 * d)
    mod_s = mod[bsz:].reshape(1, nsmp, 6 * d)

    xp = x_prompt.reshape(bsz * seq, d)
    xr, gr, ga, gb, lf, q_aug, k_aug, kt, vt, vtb = _in_proj(xp, mod_p, seq, ROW_TILE, n1, w_all,
                                                             bf_pad, True)
    logf_p = lf[:, :N_HEADS].reshape(bsz, seq, N_HEADS)
    a_in, h_last_p = _rglru_seq(xr, gr, bsz, seq, rg)
    xs = x_sample.reshape(nsmp, d)
    xr_s, gr_s, ga_s, gb_s, lf_s, q_s, k_s, v_s = _in_proj(xs, mod_s, nsmp, nsmp, n1, w_all,
                                                           bf_pad, False)
    logf_s = lf_s[:, :N_HEADS]
    a_in_s, h_s = _rglru_step(xr_s, gr_s, jnp.transpose(state_conv[0], (1, 0, 2)), state_rglru[0],
                              rg, past_len == 0)

    att, att_s = _fox_attention(
        q_aug, k_aug, vtb, bsz, seq,
        q_s.astype(F32).reshape(nsmp, *hd), k_s.reshape(nsmp, *hd), v_s.reshape(nsmp, *hd),
        logf_s.reshape(nsmp, N_HEADS, 1),
        jnp.transpose(cache_k, (0, 1, 3, 4, 2)), jnp.transpose(cache_v, (0, 1, 3, 4, 2)),
        jnp.transpose(cache_logf, (0, 1, 3, 2)), page_table)

    k_p = jnp.transpose(kt.reshape(1, bsz, *hd, seq), (0, 1, 4, 2, 3))
    v_p = jnp.transpose(vt.reshape(1, bsz, *hd, seq), (0, 1, 4, 2, 3))
    x1, h2 = _merge(xp, a_in, att, ga, gb, mod_p, seq, ROW_TILE, n2, wa, wb, wo)
    y_p = _ffn(x1, h2, mod_p, seq, ROW_TILE, nf, wfi, wfo).reshape(bsz, seq, d)
    last = [seq - ROW_TILE + (ROW_TILE // SUBLANES - (CONV_W - 1) + i) * SUBLANES + SUBLANES - 1
            for i in range(CONV_W - 1)]
    xr3 = xr.reshape(bsz, seq, d)
    conv_p = jnp.stack([xr3[:, r] for r in last], axis=1)

    x1_s, h2_s = _merge(xs, a_in_s, att_s.reshape(nsmp, d).astype(BF16), ga_s, gb_s, mod_s, nsmp,
                        nsmp, n2, wa, wb, wo)
    y_s = _ffn(x1_s, h2_s, mod_s, nsmp, nsmp, nf, wfi, wfo).reshape(nsmp, 1, d)
    conv_s = jnp.concatenate([state_conv[0][:, 1:], xr_s[:, None, :]], axis=1)

    return (y_p, y_s,
            k_p, v_p, logf_p[None],
            conv_p[None], h_last_p.reshape(1, bsz, d),
            k_s.reshape(1, nsmp, 1, *hd), v_s.reshape(1, nsmp, 1, *hd),
            logf_s.reshape(1, nsmp, 1, N_HEADS),
            conv_s[None], h_s[None])
```

```python
import functools

import numpy as np
import jax
import jax.numpy as jnp
from jax import lax
from jax.experimental import pallas as pl
from jax.experimental.pallas import tpu as pltpu

F32 = jnp.float32
BF16 = jnp.bfloat16

D_MODEL = 1024
N_HEADS = 16
HEAD_DIM = 64
N_BLK = 16
CONV_W = 4
RG_C = 8.0
PAGE_SIZE = 128
D_FF = 2816
EPS = 1e-6

LANES = 128
SUBLANES = 8
MXU_DIM = 256
VMEM_LIMIT = 56 * 1024 * 1024

ROW_TILE = 256
DENSE_TILE = 512
ATT_TILE = 512
ATT_HEADS = 8
PAGES_PER_STEP = 8

RG_CHUNK = 128
N_SEG = 7
W_ALL_COLS = N_SEG * D_MODEL + LANES

AUG = HEAD_DIM
N_SPLIT = 3
ONES_LANE = N_SPLIT * N_HEADS


def _params(*sem):
    return pltpu.CompilerParams(dimension_semantics=sem, vmem_limit_bytes=VMEM_LIMIT)


def _resident(shape):
    nd = len(shape)
    return pl.BlockSpec(shape, lambda *_: (0,) * nd, pipeline_mode=pl.Buffered(1))


def _sigmoid(x):
    return 1.0 / (1.0 + jnp.exp(-x))


def _log_sigmoid(x):
    return jnp.minimum(x, 0.0) - jnp.log1p(jnp.exp(-jnp.abs(x)))


def _softplus(x):
    return jnp.maximum(x, 0.0) + jnp.log1p(jnp.exp(-jnp.abs(x)))


def _rms(x, g):
    return x * lax.rsqrt(jnp.mean(x * x, axis=-1, keepdims=True) + EPS) * g


def _ada_kernel(c_ref, w_ref, b_ref, o_ref):
    c = c_ref[...]
    s = (c * _sigmoid(c)).astype(BF16)
    o_ref[...] = jnp.dot(s, w_ref[...].astype(BF16), preferred_element_type=F32) + b_ref[...]


def _ada(c, w, b):
    n, d = c.shape
    cols = w.shape[1]
    return pl.pallas_call(
        _ada_kernel,
        grid=(cols // d,),
        in_specs=[pl.BlockSpec((n, d), lambda j: (0, 0)),
                  pl.BlockSpec((d, d), lambda j: (0, j)),
                  pl.BlockSpec((1, d), lambda j: (0, j))],
        out_specs=pl.BlockSpec((n, d), lambda j: (0, j)),
        out_shape=jax.ShapeDtypeStruct((n, cols), F32),
        compiler_params=_params("arbitrary"),
        name="ada_modulation",
    )(c, w, b.reshape(1, cols))


def _mod_spec(rows_per_group, tm, r, chunk):
    tiles = rows_per_group // tm
    return pl.BlockSpec((1, r, D_MODEL), lambda i: (i // tiles, 0, chunk))


def _split3(x):
    hi = x.astype(BF16).astype(F32)
    r = x - hi
    mid = r.astype(BF16).astype(F32)
    return hi, mid, r - mid


def _in_proj_kernel(*refs, prompt, tiles_per_seq):
    if prompt:
        (x_ref, g_ref, sh_ref, sc_ref, w_ref, bf_ref, tri_ref, pq_ref, pk_ref, perm_ref, unperm_ref,
         cw_ref, cb_ref, wr_ref, wi_ref, br_ref, bi_ref, lam_ref,
         ga_ref, gb_ref, lf_ref, qa_ref, ka_ref, kt_ref, vt_ref, vtb_ref, ain_ref, hl_ref, ct_ref,
         fcar, xbuf, prev_s, hcar) = refs
    else:
        (x_ref, g_ref, sh_ref, sc_ref, w_ref, bf_ref,
         xr_ref, gr_ref, ga_ref, gb_ref, lf_ref, q_ref, k_ref, v_ref) = refs
    x = x_ref[...]
    h = (_rms(x, g_ref[...]) * (1.0 + sc_ref[0]) + sh_ref[0]).astype(BF16)

    def seg(s, lhs=h):
        return jnp.dot(lhs, w_ref[:, s * D_MODEL:(s + 1) * D_MODEL], preferred_element_type=F32)

    if prompt:
        first = pl.program_id(0) % tiles_per_seq == 0

        @pl.when(first)
        def _():
            fcar[...] = jnp.zeros_like(fcar)
            prev_s[...] = jnp.zeros_like(prev_s)
            hcar[...] = jnp.zeros_like(hcar)

        hs = jnp.dot(perm_ref[...], h, preferred_element_type=F32).astype(BF16)
        xr = seg(0, hs)
        gr = seg(1, hs)
        n_rows = xr.shape[0]
        ct_ref[0] = jnp.concatenate(
            [xr[n_rows - 1 - i * SUBLANES:n_rows - i * SUBLANES, :]
             for i in reversed(range(CONV_W - 1))], axis=0)

        def rglru_chunk(c):
            lanes = slice(c * RG_CHUNK, (c + 1) * RG_CHUNK)
            ain_ref[:, lanes], hl_ref[0, :, lanes] = _rglru_tile(
                xr[:, lanes], gr[:, lanes], first, c, unperm_ref, cw_ref, cb_ref, wr_ref, wi_ref,
                br_ref, bi_ref, lam_ref, xbuf, prev_s, hcar)
        n_chunks = D_MODEL // RG_CHUNK
    else:
        xr_ref[...] = seg(0)
        gr_ref[...] = seg(1)
        n_chunks = 0

    half = D_MODEL // 2
    halves = {}
    for n, (s, i) in enumerate((s, i) for s in range(2, N_SEG) for i in range(2)):
        if n < n_chunks:
            rglru_chunk(n)
        cols = slice(s * D_MODEL + i * half, s * D_MODEL + (i + 1) * half)
        halves[s, i] = jnp.dot(h, w_ref[:, cols], preferred_element_type=F32)
    q, k, v, ga, gb = (jnp.concatenate([halves[s, 0], halves[s, 1]], axis=1)
                       for s in range(2, N_SEG))
    q = (q * (HEAD_DIM ** -0.5)).astype(BF16)
    ga_ref[...] = ga
    gb_ref[...] = gb
    fl = jnp.dot(h, w_ref[:, N_SEG * D_MODEL:], preferred_element_type=F32)
    lf = _log_sigmoid(fl + bf_ref[...])
    lf_ref[...] = lf
    if not prompt:
        q_ref[...] = q
        k_ref[...] = k
        v_ref[...] = v
        return

    tm = x.shape[0]
    kt_ref[0] = k.T
    vt = v.T
    vt_ref[0] = vt
    vtb_ref[0] = vt.astype(BF16)

    lane = lax.broadcasted_iota(jnp.int32, (1, LANES), 1)
    tri = tri_ref[...]
    cum = fcar[...]
    for piece in _split3(jnp.where(lane < N_HEADS, lf, 0.0)):
        cum = cum + jnp.dot(tri, piece.astype(BF16), preferred_element_type=F32)
    fcar[...] = cum[tm - 1:tm, :]

    c_hi, c_mid, c_lo = _split3(cum)
    faug = jnp.where(lane < N_HEADS, c_hi,
                     jnp.where(lane < 2 * N_HEADS, pltpu.roll(c_mid, N_HEADS, axis=1),
                               jnp.where(lane < ONES_LANE, pltpu.roll(c_lo, 2 * N_HEADS, axis=1),
                                         jnp.where(lane == ONES_LANE, 1.0, 0.0)))).astype(BF16)
    kb = k.astype(BF16)
    for j in range(N_HEADS // 2):
        pair = slice(j * LANES, (j + 1) * LANES)
        out = slice(j * MXU_DIM, (j + 1) * MXU_DIM)
        qa_ref[:, out] = jnp.dot(jnp.concatenate([q[:, pair], faug], axis=1), pq_ref[j],
                                 preferred_element_type=F32).astype(BF16)
        ka_ref[:, out] = jnp.dot(jnp.concatenate([kb[:, pair], faug], axis=1), pk_ref[j],
                                 preferred_element_type=F32).astype(BF16)


def _scan_order(tile):
    seg_len = tile // SUBLANES
    n = np.arange(tile)
    perm = np.zeros((tile, tile), np.float32)
    perm[n, (n % SUBLANES) * seg_len + n // SUBLANES] = 1.0
    return perm


def _placement_matrices():
    n_pairs = N_HEADS // 2
    pq = np.zeros((n_pairs, MXU_DIM, MXU_DIM), np.float32)
    pk = np.zeros((n_pairs, MXU_DIM, MXU_DIM), np.float32)
    for j in range(n_pairs):
        for s in range(2):
            head = 2 * j + s
            for d in range(HEAD_DIM):
                pq[j, s * HEAD_DIM + d, s * LANES + d] = 1.0
                pk[j, s * HEAD_DIM + d, s * LANES + d] = 1.0
            cb = s * LANES + AUG
            for i in range(N_SPLIT):
                pq[j, LANES + i * N_HEADS + head, cb + i] = 1.0
                pq[j, LANES + ONES_LANE, cb + N_SPLIT + i] = 1.0
                pk[j, LANES + ONES_LANE, cb + i] = 1.0
                pk[j, LANES + i * N_HEADS + head, cb + N_SPLIT + i] = -1.0
    return jnp.asarray(pq, BF16), jnp.asarray(pk, BF16)


def _in_proj(x, mod, rows_per_group, tm, norm_g, w_all, bf_pad, rg=None):
    prompt = rg is not None
    m = x.shape[0]
    r = mod.shape[1]
    row = lambda i: (i, 0)
    f32o = jax.ShapeDtypeStruct((m, D_MODEL), F32)
    bf16o = jax.ShapeDtypeStruct((m, D_MODEL), BF16)
    blk = pl.BlockSpec((tm, D_MODEL), row)
    in_specs = [blk, _resident((1, D_MODEL)),
                _mod_spec(rows_per_group, tm, r, 0), _mod_spec(rows_per_group, tm, r, 1),
                _resident((D_MODEL, W_ALL_COLS)), _resident((1, LANES))]
    operands = [x, norm_g, mod, mod, w_all, bf_pad]
    lf_spec = pl.BlockSpec((tm, LANES), row)
    lf_shape = jax.ShapeDtypeStruct((m, LANES), F32)
    scratch = []
    if prompt:
        tiles = rows_per_group // tm
        n_seq = m // rows_per_group
        halo = (CONV_W - 1) * SUBLANES
        tri = jnp.asarray(np.tril(np.ones((tm, tm), np.float32)), BF16)
        pq, pk = _placement_matrices()
        perm = _scan_order(tm)
        consts = [tri, pq, pk, jnp.asarray(perm, BF16), jnp.asarray(perm.T, BF16)]
        in_specs += [_resident(c.shape) for c in consts] + [_resident(p.shape) for p in rg]
        operands += consts + list(rg)
        wide = pl.BlockSpec((tm, 2 * D_MODEL), row)
        time_on_lanes = pl.BlockSpec((1, D_MODEL, tm), lambda i: (i // tiles, 0, i % tiles))
        per_seq = lambda rows: pl.BlockSpec((1, rows, D_MODEL), lambda i: (i // tiles, 0, 0))
        out_specs = ([blk, blk, lf_spec, wide, wide] + [time_on_lanes] * 3
                     + [blk, per_seq(1), per_seq(CONV_W - 1)])
        out_shape = ([f32o, f32o, lf_shape] + [jax.ShapeDtypeStruct((m, 2 * D_MODEL), BF16)] * 2
                     + [jax.ShapeDtypeStruct((n_seq, D_MODEL, rows_per_group), F32)] * 2
                     + [jax.ShapeDtypeStruct((n_seq, D_MODEL, rows_per_group), BF16), bf16o,
                        jax.ShapeDtypeStruct((n_seq, 1, D_MODEL), F32),
                        jax.ShapeDtypeStruct((n_seq, CONV_W - 1, D_MODEL), F32)])
        scratch = [pltpu.VMEM((1, LANES), F32), pltpu.VMEM((tm + halo, D_MODEL), F32),
                   pltpu.VMEM((halo, D_MODEL), F32), pltpu.VMEM((SUBLANES, D_MODEL), F32)]
    else:
        out_specs = [blk] * 4 + [lf_spec] + [blk] * 3
        out_shape = [f32o] * 4 + [lf_shape, bf16o, f32o, f32o]
    return pl.pallas_call(
        functools.partial(_in_proj_kernel, prompt=prompt, tiles_per_seq=rows_per_group // tm),
        grid=(m // tm,),
        in_specs=in_specs, out_specs=out_specs, out_shape=out_shape, scratch_shapes=scratch,
        compiler_params=_params("arbitrary"),
        name="in_proj_prompt" if prompt else "in_proj_sample",
    )(*operands)


def _block_gates(xc, wr_ref, wi_ref, br, bi):
    xcb = xc.astype(BF16)
    n = D_MODEL // RG_CHUNK
    pr = [jnp.dot(xcb[:, c * RG_CHUNK:(c + 1) * RG_CHUNK], wr_ref[c], preferred_element_type=F32)
          for c in range(n)]
    pi = [jnp.dot(xcb[:, c * RG_CHUNK:(c + 1) * RG_CHUNK], wi_ref[c], preferred_element_type=F32)
          for c in range(n)]
    gate_r = _sigmoid(jnp.concatenate(pr, axis=1) + br)
    gate_i = _sigmoid(jnp.concatenate(pi, axis=1) + bi)
    return gate_r, gate_i


def _decay_and_input(xc, gate_r, gate_i, lam, is_start):
    log_a = -RG_C * gate_r * _softplus(-lam)
    a = jnp.exp(log_a)
    mult = jnp.sqrt(jnp.tanh(-log_a) * (a * a + 1.0))
    if is_start is not None:
        mult = jnp.where(is_start, 1.0, mult)
    return a, mult * gate_i * xc


def _rglru_tile(x, gr, first, c, unperm_ref, cw_ref, cb_ref, wr_ref, wi_ref, br_ref, bi_ref,
                lam_ref, xbuf, prev_s, hcar):
    tt, width = x.shape
    lanes = slice(c * width, (c + 1) * width)
    seg_len = tt // SUBLANES
    halo = (CONV_W - 1) * SUBLANES
    sub = lax.broadcasted_iota(jnp.int32, (SUBLANES, 1), 0)
    for i in range(CONV_W - 1):
        rows = slice(i * SUBLANES, (i + 1) * SUBLANES)
        cur = x[tt - halo + i * SUBLANES:tt - halo + (i + 1) * SUBLANES, :]
        xbuf[rows, lanes] = jnp.where(sub == 0, pltpu.roll(prev_s[rows, lanes], 1, axis=0),
                                      pltpu.roll(cur, 1, axis=0))
    prev_s[:, lanes] = x[tt - halo:, :]
    xbuf[halo:, lanes] = x
    xc = cb_ref[:, lanes] + cw_ref[CONV_W - 1:CONV_W, lanes] * x
    for j in range(CONV_W - 1):
        back = CONV_W - 1 - j
        xc = xc + cw_ref[j:j + 1, lanes] * xbuf[pl.ds(halo - back * SUBLANES, tt), lanes]

    xcb = xc.astype(BF16)
    gate_r = _sigmoid(jnp.dot(xcb, wr_ref[c], preferred_element_type=F32) + br_ref[:, lanes])
    gate_i = _sigmoid(jnp.dot(xcb, wi_ref[c], preferred_element_type=F32) + bi_ref[:, lanes])
    is_start = (lax.broadcasted_iota(jnp.int32, (tt, 1), 0) + jnp.where(first, 0, 1)) == 0
    a, u = _decay_and_input(xc, gate_r, gate_i, lam_ref[:, lanes], is_start)

    hh = jnp.zeros((SUBLANES, width), F32)
    pp = jnp.ones((SUBLANES, width), F32)
    local, decay = [], []
    for g in range(seg_len):
        rows = slice(g * SUBLANES, (g + 1) * SUBLANES)
        hh = a[rows, :] * hh + u[rows, :]
        pp = a[rows, :] * pp
        local.append(hh)
        decay.append(pp)
    cy = hcar[0:1, lanes]
    carries = []
    for s in range(SUBLANES):
        carries.append(cy)
        cy = hh[s:s + 1, :] + pp[s:s + 1, :] * cy
    hcar[0:1, lanes] = cy
    cin = jnp.concatenate(carries, axis=0)
    hr = jnp.concatenate([local[g] + decay[g] * cin for g in range(seg_len)], axis=0)
    gated = (hr * jax.nn.gelu(gr)).astype(BF16)
    return jnp.dot(unperm_ref[...], gated, preferred_element_type=F32).astype(BF16), cy


def _rglru_step_kernel(xr_ref, gr_ref, st_ref, h0_ref, cw_ref, cb_ref, wr_ref, wi_ref, br_ref,
                       bi_ref, lam_ref, o_ref, h_ref, *, at_start):
    xr = xr_ref[...]
    xc = cb_ref[...] + cw_ref[CONV_W - 1:CONV_W, :] * xr
    for j in range(CONV_W - 1):
        xc = xc + cw_ref[j:j + 1, :] * st_ref[j]
    gate_r, gate_i = _block_gates(xc, wr_ref, wi_ref, br_ref[...], bi_ref[...])
    a, u = _decay_and_input(xc, gate_r, gate_i, lam_ref[...], True if at_start else None)
    h = a * h0_ref[...] + u
    h_ref[...] = h
    o_ref[...] = (h * jax.nn.gelu(gr_ref[...])).astype(BF16)


def _rglru_step(xr, gr, state_t, h0, rg, at_start):
    n = xr.shape[0]
    full = lambda shape: pl.BlockSpec(shape, lambda i: (0,) * len(shape))
    nchunk = D_MODEL // RG_CHUNK
    return pl.pallas_call(
        functools.partial(_rglru_step_kernel, at_start=at_start),
        grid=(1,),
        in_specs=[full((n, D_MODEL)), full((n, D_MODEL)), full((CONV_W - 1, n, D_MODEL)),
                  full((n, D_MODEL)), full((CONV_W, D_MODEL)), full((1, D_MODEL)),
                  full((nchunk, RG_CHUNK, RG_CHUNK)), full((nchunk, RG_CHUNK, RG_CHUNK)),
                  full((1, D_MODEL)), full((1, D_MODEL)), full((1, D_MODEL))],
        out_specs=[full((n, D_MODEL)), full((n, D_MODEL))],
        out_shape=[jax.ShapeDtypeStruct((n, D_MODEL), BF16),
                   jax.ShapeDtypeStruct((n, D_MODEL), F32)],
        compiler_params=_params("arbitrary"),
        name="rglru_sample",
    )(xr, gr, state_t, h0, *rg)


def _prompt_update(q_ref, k_ref, vt_ref, m_s, l_s, acc_s, masked, tq, tk):
    for h in range(ATT_HEADS):
        lanes = slice(h * LANES, (h + 1) * LANES)
        st = lax.dot_general(k_ref[:, lanes], q_ref[:, lanes], (((1,), (1,)), ((), ())),
                             preferred_element_type=F32)
        if masked:
            kpos = lax.broadcasted_iota(jnp.int32, (tk, tq), 0)
            qpos = lax.broadcasted_iota(jnp.int32, (tk, tq), 1)
            st = jnp.where(kpos <= qpos, st, -jnp.inf)
        m_prev = m_s[h]
        m_new = jnp.maximum(m_prev, jnp.max(st, axis=0, keepdims=True))
        alpha = jnp.exp(m_prev - m_new)
        p = jnp.exp(st - m_new)
        l_s[h] = alpha * l_s[h] + jnp.sum(p, axis=0, keepdims=True)
        pv = jnp.dot(vt_ref[0, h * HEAD_DIM:(h + 1) * HEAD_DIM, :], p.astype(BF16),
                     preferred_element_type=F32)
        acc_s[h] = alpha * acc_s[h] + pv
        m_s[h] = m_new


def _paged_update(valid, k_refs, v_refs, lf_refs, qb_s, m_s, l_s, acc_s, car_s):
    g = len(k_refs)
    lane = lax.broadcasted_iota(jnp.int32, (N_HEADS, PAGE_SIZE), 1)

    carry_in = car_s[...]
    carry = carry_in
    bias = []
    for j in range(g):
        pref = lf_refs[j][0, 0]
        d = 1
        while d < PAGE_SIZE:
            pref = pref + jnp.where(lane >= d, pltpu.roll(pref, d, axis=1), 0.0)
            d *= 2
        bias.append(carry + pref)
        carry = carry + pref[:, PAGE_SIZE - 1:PAGE_SIZE]
    car_s[...] = jnp.where(valid, carry, carry_in)

    for hg in range(N_HEADS // SUBLANES):
        heads = range(hg * SUBLANES, (hg + 1) * SUBLANES)
        grp = slice(hg * SUBLANES, (hg + 1) * SUBLANES)
        s = jnp.concatenate(
            [jnp.concatenate([jnp.sum(k_refs[j][0, 0, h] * qb_s[h], axis=0, keepdims=True)
                              for h in heads], axis=0) - bias[j][grp, :]
             for j in range(g)], axis=1)
        m_prev = m_s[grp, :]
        m_grp = jnp.where(valid, jnp.maximum(m_prev, jnp.max(s, axis=1, keepdims=True)), m_prev)
        alpha = jnp.where(valid, jnp.exp(m_prev - m_grp), 1.0)
        pr = jnp.where(valid, jnp.exp(s - m_grp), 0.0)
        l_s[grp, :] = alpha * l_s[grp, :] + jnp.sum(pr, axis=1, keepdims=True)
        m_s[grp, :] = m_grp
        for i, h in enumerate(heads):
            a_h = acc_s[h] * alpha[i:i + 1, :]
            for j in range(g):
                a_h = a_h + pr[i:i + 1, j * PAGE_SIZE:(j + 1) * PAGE_SIZE] * v_refs[j][0, 0, h]
            acc_s[h] = a_h


def _paged_finish(qr_ref, kc_ref, vc_ref, lc_ref, o_ref, m_s, l_s, acc_s, car_s):
    m_past = m_s[...]
    s_c = jnp.sum(qr_ref[0] * kc_ref[0], axis=1, keepdims=True) - (car_s[...] + lc_ref[0])
    m_f = jnp.maximum(m_past, s_c)
    al = jnp.exp(m_past - m_f)
    p_c = jnp.exp(s_c - m_f)
    inv_l = 1.0 / (al * l_s[...] + p_c)
    for h in range(N_HEADS):
        past = jnp.sum(acc_s[h], axis=1, keepdims=True) * al[h:h + 1, :]
        o_ref[0, h] = (past + p_c[h:h + 1, :] * vc_ref[0, h]) * inv_l[h:h + 1, :]


def _fox_kernel(qt_ref, kt_ref, pt_ref, q_ref, k_ref, vt_ref, qr_ref, qc_ref, kc_ref, vc_ref,
                lc_ref, *rest, tq, tk, g, n_paged_steps, steps_per_seq):
    k_refs = rest[0:g]
    v_refs = rest[g:2 * g]
    lf_refs = rest[2 * g:3 * g]
    o_ref, os_ref = rest[3 * g], rest[3 * g + 1]
    m_s, l_s, acc_s, qb_s, pm_s, pl_s, pacc_s, car_s = rest[3 * g + 2:]
    t = pl.program_id(2)
    qi = qt_ref[t]
    ki = kt_ref[t]
    step = (pl.program_id(0) * pl.num_programs(1) + pl.program_id(1)) * pl.num_programs(2) + t
    valid = step < n_paged_steps
    page_step = jnp.minimum(step, n_paged_steps - 1) % steps_per_seq

    @pl.when(ki == 0)
    def _():
        m_s[...] = jnp.full_like(m_s, -jnp.inf)
        l_s[...] = jnp.zeros_like(l_s)
        acc_s[...] = jnp.zeros_like(acc_s)

    @pl.when(jnp.logical_and(valid, page_step == 0))
    def _():
        qb_s[...] = jnp.broadcast_to(qc_ref[0], qb_s.shape)
        pm_s[...] = jnp.full_like(pm_s, -jnp.inf)
        pl_s[...] = jnp.zeros_like(pl_s)
        pacc_s[...] = jnp.zeros_like(pacc_s)
        car_s[...] = jnp.zeros_like(car_s)

    def both(masked):
        _prompt_update(q_ref, k_ref, vt_ref, m_s, l_s, acc_s, masked, tq, tk)
        _paged_update(valid, k_refs, v_refs, lf_refs, qb_s, pm_s, pl_s, pacc_s, car_s)

    @pl.when(ki < qi)
    def _():
        both(False)

    @pl.when(ki == qi)
    def _():
        both(True)
        ot = jnp.concatenate([acc_s[h] / l_s[h] for h in range(ATT_HEADS)], axis=0)
        o_ref[...] = ot.T.astype(BF16)

    @pl.when(jnp.logical_and(valid, page_step == steps_per_seq - 1))
    def _():
        _paged_finish(qr_ref, kc_ref, vc_ref, lc_ref, os_ref, pm_s, pl_s, pacc_s, car_s)


def _fox_attention(q_aug, k_aug, vt, n_batch, seq, q, k_cur, v_cur, lf_cur, cache_kt, cache_vt,
                   cache_lft, page_table):
    tq = tk = ATT_TILE
    nq = seq // tq
    pairs = [(i, j) for i in range(nq) for j in range(i + 1)]
    qt = jnp.asarray(np.array([p[0] for p in pairs], np.int32))
    kt = jnp.asarray(np.array([p[1] for p in pairs], np.int32))
    groups = N_HEADS // ATT_HEADS
    wide = ATT_HEADS * LANES
    narrow = ATT_HEADS * HEAD_DIM

    n, n_pages = page_table.shape
    g = PAGES_PER_STEP
    steps_per_seq = n_pages // g
    n_paged_steps = n * steps_per_seq
    assert n_paged_steps <= n_batch * groups * len(pairs), "not enough grid steps for the cache pages"

    def seq_and_step(b, gi, t):
        s = jnp.minimum((b * groups + gi) * len(pairs) + t, n_paged_steps - 1)
        return s // steps_per_seq, s % steps_per_seq

    def per_seq(shape):
        zeros = (0,) * len(shape)
        return pl.BlockSpec((1,) + shape,
                            lambda b, gi, t, qt, kt, pt: (seq_and_step(b, gi, t)[0],) + zeros)

    def page(tail, j):
        zeros = (0,) * len(tail)

        def index(b, gi, t, qt, kt, pt):
            sq, st = seq_and_step(b, gi, t)
            return (0, pt[sq, st * g + j]) + zeros
        return pl.BlockSpec((1, 1) + tail, index)

    row, col = per_seq((N_HEADS, HEAD_DIM)), per_seq((N_HEADS, HEAD_DIM, 1))
    grid_spec = pltpu.PrefetchScalarGridSpec(
        num_scalar_prefetch=3,
        grid=(n_batch, groups, len(pairs)),
        in_specs=([pl.BlockSpec((tq, wide), lambda b, gi, t, qt, kt, pt: (b * nq + qt[t], gi)),
                   pl.BlockSpec((tk, wide), lambda b, gi, t, qt, kt, pt: (b * nq + kt[t], gi)),
                   pl.BlockSpec((1, narrow, tk), lambda b, gi, t, qt, kt, pt: (b, gi, kt[t])),
                   row, col, row, col, per_seq((N_HEADS, 1))]
                  + [page((N_HEADS, HEAD_DIM, PAGE_SIZE), j) for j in range(g)]
                  + [page((N_HEADS, HEAD_DIM, PAGE_SIZE), j) for j in range(g)]
                  + [page((N_HEADS, PAGE_SIZE), j) for j in range(g)]),
        out_specs=[pl.BlockSpec((tq, narrow), lambda b, gi, t, qt, kt, pt: (b * nq + qt[t], gi)),
                   col],
        scratch_shapes=[pltpu.VMEM((ATT_HEADS, 1, tq), F32), pltpu.VMEM((ATT_HEADS, 1, tq), F32),
                        pltpu.VMEM((ATT_HEADS, HEAD_DIM, tq), F32),
                        pltpu.VMEM((N_HEADS, HEAD_DIM, PAGE_SIZE), F32),
                        pltpu.VMEM((N_HEADS, 1), F32), pltpu.VMEM((N_HEADS, 1), F32),
                        pltpu.VMEM((N_HEADS, HEAD_DIM, PAGE_SIZE), F32),
                        pltpu.VMEM((N_HEADS, 1), F32)],
    )
    return pl.pallas_call(
        functools.partial(_fox_kernel, tq=tq, tk=tk, g=g, n_paged_steps=n_paged_steps,
                          steps_per_seq=steps_per_seq),
        grid_spec=grid_spec,
        out_shape=[jax.ShapeDtypeStruct((n_batch * seq, D_MODEL), BF16),
                   jax.ShapeDtypeStruct((n, N_HEADS, HEAD_DIM, 1), F32)],
        compiler_params=_params("arbitrary", "arbitrary", "arbitrary"),
        name="fox_attention",
    )(qt, kt, page_table, q_aug, k_aug, vt, q, q[..., None], k_cur, v_cur[..., None], lf_cur,
      *([cache_kt] * g), *([cache_vt] * g), *([cache_lft] * g))


def _merge_kernel(x_ref, a_ref, b_ref, ga_ref, gb_ref, g1_ref, sh_ref, sc_ref, n2_ref,
                  wa_ref, wb_ref, wo_ref, x1_ref, h2_ref):
    ya = jnp.dot(a_ref[...], wa_ref[...], preferred_element_type=F32)
    yb = jnp.dot(b_ref[...], wb_ref[...], preferred_element_type=F32)
    mix = (_sigmoid(ga_ref[...]) * ya + _sigmoid(gb_ref[...]) * yb).astype(BF16)
    x1 = x_ref[...] + g1_ref[0] * jnp.dot(mix, wo_ref[...], preferred_element_type=F32)
    x1_ref[...] = x1
    h2_ref[...] = (_rms(x1, n2_ref[...]) * (1.0 + sc_ref[0]) + sh_ref[0]).astype(BF16)


def _merge(x, a_in, b_in, ga, gb, mod, rows_per_group, tm, norm_g, wa, wb, wo):
    m = x.shape[0]
    r = mod.shape[1]
    blk = pl.BlockSpec((tm, D_MODEL), lambda i: (i, 0))
    sq = _resident((D_MODEL, D_MODEL))
    return pl.pallas_call(
        _merge_kernel,
        grid=(m // tm,),
        in_specs=[blk, blk, blk, blk, blk,
                  _mod_spec(rows_per_group, tm, r, 2), _mod_spec(rows_per_group, tm, r, 3),
                  _mod_spec(rows_per_group, tm, r, 4), _resident((1, D_MODEL)), sq, sq, sq],
        out_specs=[blk, blk],
        out_shape=[jax.ShapeDtypeStruct((m, D_MODEL), F32), jax.ShapeDtypeStruct((m, D_MODEL), BF16)],
        compiler_params=_params("arbitrary"),
        name="merge_proj",
    )(x, a_in, b_in, ga, gb, mod, mod, mod, norm_g, wa, wb, wo)


def _ffn_kernel(x1_ref, h2_ref, g2_ref, nf_ref, wi_ref, wo_ref, y_ref):
    h2 = h2_ref[...]
    gu = jnp.dot(h2, wi_ref[:, 0:D_FF], preferred_element_type=F32)
    up = jnp.dot(h2, wi_ref[:, D_FF:2 * D_FF], preferred_element_type=F32)
    act = (gu * _sigmoid(gu) * up).astype(BF16)
    x2 = x1_ref[...] + g2_ref[0] * jnp.dot(act, wo_ref[...], preferred_element_type=F32)
    y_ref[...] = _rms(x2, nf_ref[...])


def _ffn(x1, h2, mod, rows_per_group, tm, normf_g, w_in, w_out):
    m = x1.shape[0]
    r = mod.shape[1]
    blk = pl.BlockSpec((tm, D_MODEL), lambda i: (i, 0))
    return pl.pallas_call(
        _ffn_kernel,
        grid=(m // tm,),
        in_specs=[blk, blk, _mod_spec(rows_per_group, tm, r, 5), _resident((1, D_MODEL)),
                  _resident((D_MODEL, 2 * D_FF)), _resident((D_FF, D_MODEL))],
        out_specs=blk,
        out_shape=jax.ShapeDtypeStruct((m, D_MODEL), F32),
        compiler_params=_params("arbitrary"),
        name="ffn_final",
    )(x1, h2, mod, normf_g, w_in, w_out)


def _pair_blocks(w):
    per = RG_CHUNK // (D_MODEL // N_BLK)
    bw = D_MODEL // N_BLK
    w = w.reshape(N_BLK // per, per, bw, bw)
    eye = jnp.eye(per, dtype=w.dtype)
    return jnp.einsum("cpij,pq->cpiqj", w, eye).reshape(N_BLK // per, RG_CHUNK, RG_CHUNK).astype(BF16)


def kernel(x_prompt, x_sample, c_prompt, c_sample, cache_k, cache_v, cache_logf, state_conv,
           state_rglru, page_table, ada_w, ada_b, norm1_g, norm2_g, normf_g, w_in, b_f, conv_w,
           conv_b, rg_wr, rg_br, rg_wi, rg_bi, rg_lambda, w_proj_a, w_proj_b, w_o, w_ffn_in,
           w_ffn_out):
    depth = ada_w.shape[0]
    assert depth == 1, "single-layer trunk"
    bsz, seq, d = x_prompt.shape
    nsmp = x_sample.shape[0]
    assert x_sample.shape[1] == 1 and d == D_MODEL
    past_len = page_table.shape[1] * PAGE_SIZE
    attn_w = N_HEADS * HEAD_DIM
    hd = (N_HEADS, HEAD_DIM)

    wl = w_in[0]
    cut = 2 * D_MODEL + 3 * attn_w
    w_all = jnp.concatenate(
        [wl[:, :cut], wl[:, cut + N_HEADS:], jnp.pad(wl[:, cut:cut + N_HEADS], ((0, 0), (0, LANES - N_HEADS)))],
        axis=1).astype(BF16)
    bf_pad = jnp.pad(b_f[0], (0, LANES - N_HEADS)).reshape(1, LANES)
    rg = (conv_w[0], conv_b[0].reshape(1, d), _pair_blocks(rg_wr[0]), _pair_blocks(rg_wi[0]),
          rg_br[0].reshape(1, d), rg_bi[0].reshape(1, d), rg_lambda[0].reshape(1, d))
    wa, wb, wo = (w[0].astype(BF16) for w in (w_proj_a, w_proj_b, w_o))
    wfi, wfo = w_ffn_in[0].astype(BF16), w_ffn_out[0].astype(BF16)
    n1, n2, nf = norm1_g[0].reshape(1, d), norm2_g[0].reshape(1, d), normf_g.reshape(1, d)

    mod = _ada(jnp.concatenate([c_prompt, c_sample], axis=0), ada_w[0], ada_b[0])
    mod_p = mod[:bsz].reshape(bsz, 1, 6 * d)
    mod_s = mod[bsz:].reshape(1, nsmp, 6 * d)

    xp = x_prompt.reshape(bsz * seq, d)
    ga, gb, lf, q_aug, k_aug, kt, vt, vtb, a_in, h_last_p, conv_p = _in_proj(
        xp, mod_p, seq, ROW_TILE, n1, w_all, bf_pad, rg)
    logf_p = lf[:, :N_HEADS].reshape(bsz, seq, N_HEADS)
    xs = x_sample.reshape(nsmp, d)
    xr_s, gr_s, ga_s, gb_s, lf_s, q_s, k_s, v_s = _in_proj(xs, mod_s, nsmp, nsmp, n1, w_all, bf_pad)
    logf_s = lf_s[:, :N_HEADS]
    a_in_s, h_s = _rglru_step(xr_s, gr_s, jnp.transpose(state_conv[0], (1, 0, 2)), state_rglru[0],
                              rg, past_len == 0)

    att, att_s = _fox_attention(
        q_aug, k_aug, vtb, bsz, seq,
        q_s.astype(F32).reshape(nsmp, *hd), k_s.reshape(nsmp, *hd), v_s.reshape(nsmp, *hd),
        logf_s.reshape(nsmp, N_HEADS, 1),
        jnp.transpose(cache_k, (0, 1, 3, 4, 2)), jnp.transpose(cache_v, (0, 1, 3, 4, 2)),
        jnp.transpose(cache_logf, (0, 1, 3, 2)), page_table)

    k_p = jnp.transpose(kt.reshape(1, bsz, *hd, seq), (0, 1, 4, 2, 3))
    v_p = jnp.transpose(vt.reshape(1, bsz, *hd, seq), (0, 1, 4, 2, 3))
    x1, h2 = _merge(xp, a_in, att, ga, gb, mod_p, seq, DENSE_TILE, n2, wa, wb, wo)
    y_p = _ffn(x1, h2, mod_p, seq, DENSE_TILE, nf, wfi, wfo).reshape(bsz, seq, d)

    x1_s, h2_s = _merge(xs, a_in_s, att_s.reshape(nsmp, d).astype(BF16), ga_s, gb_s, mod_s, nsmp,
                        nsmp, n2, wa, wb, wo)
    y_s = _ffn(x1_s, h2_s, mod_s, nsmp, nsmp, nf, wfi, wfo).reshape(nsmp, 1, d)
    conv_s = jnp.concatenate([state_conv[0][:, 1:], xr_s[:, None, :]], axis=1)

    return (y_p, y_s,
            k_p, v_p, logf_p[None],
            conv_p[None], h_last_p.reshape(1, bsz, d),
            k_s.reshape(1, nsmp, 1, *hd), v_s.reshape(1, nsmp, 1, *hd),
            logf_s.reshape(1, nsmp, 1, N_HEADS),
            conv_s[None], h_s[None])
```

```python
import functools

import numpy as np
import jax
import jax.numpy as jnp
from jax import lax
from jax.experimental import pallas as pl
from jax.experimental.pallas import tpu as pltpu

F32 = jnp.float32
BF16 = jnp.bfloat16

D_MODEL = 1024
N_HEADS = 16
HEAD_DIM = 64
N_BLK = 16
CONV_W = 4
RG_C = 8.0
PAGE_SIZE = 128
D_FF = 2816
EPS = 1e-6

LANES = 128
SUBLANES = 8
MXU_DIM = 256
VMEM_LIMIT = 56 * 1024 * 1024

ROW_TILE = 256
DENSE_TILE = 512
ATT_TILE = 512
ATT_HEADS = 16
PAGES_PER_STEP = 16

RG_CHUNK = 128
N_SEG = 7
W_ALL_COLS = N_SEG * D_MODEL + LANES

AUG = HEAD_DIM
N_SPLIT = 3
ONES_LANE = N_SPLIT * N_HEADS


def _params(*sem):
    return pltpu.CompilerParams(dimension_semantics=sem, vmem_limit_bytes=VMEM_LIMIT)


def _resident(shape):
    nd = len(shape)
    return pl.BlockSpec(shape, lambda *_: (0,) * nd, pipeline_mode=pl.Buffered(1))


def _sigmoid(x):
    return 1.0 / (1.0 + jnp.exp(-x))


def _log_sigmoid(x):
    return jnp.minimum(x, 0.0) - jnp.log1p(jnp.exp(-jnp.abs(x)))


def _softplus(x):
    return jnp.maximum(x, 0.0) + jnp.log1p(jnp.exp(-jnp.abs(x)))


def _rms(x, g):
    return x * lax.rsqrt(jnp.mean(x * x, axis=-1, keepdims=True) + EPS) * g


def _ada_kernel(c_ref, w_ref, b_ref, o_ref):
    c = c_ref[...]
    s = (c * _sigmoid(c)).astype(BF16)
    o_ref[...] = jnp.dot(s, w_ref[...].astype(BF16), preferred_element_type=F32) + b_ref[...]


def _ada(c, w, b):
    n, d = c.shape
    cols = w.shape[1]
    return pl.pallas_call(
        _ada_kernel,
        grid=(cols // d,),
        in_specs=[pl.BlockSpec((n, d), lambda j: (0, 0)),
                  pl.BlockSpec((d, d), lambda j: (0, j)),
                  pl.BlockSpec((1, d), lambda j: (0, j))],
        out_specs=pl.BlockSpec((n, d), lambda j: (0, j)),
        out_shape=jax.ShapeDtypeStruct((n, cols), F32),
        compiler_params=_params("arbitrary"),
        name="ada_modulation",
    )(c, w, b.reshape(1, cols))


def _mod_spec(rows_per_group, tm, r, chunk):
    tiles = rows_per_group // tm
    return pl.BlockSpec((1, r, D_MODEL), lambda i: (i // tiles, 0, chunk))


def _split3(x):
    hi = x.astype(BF16).astype(F32)
    r = x - hi
    mid = r.astype(BF16).astype(F32)
    return hi, mid, r - mid


def _in_proj_kernel(*refs, prompt, tiles_per_seq):
    if prompt:
        (x_ref, g_ref, sh_ref, sc_ref, w_ref, bf_ref, tri_ref, pq_ref, pk_ref, perm_ref, unperm_ref,
         cw_ref, cb_ref, wr_ref, wi_ref, br_ref, bi_ref, lam_ref,
         ga_ref, gb_ref, lf_ref, qa_ref, ka_ref, kt_ref, vt_ref, vtb_ref, ain_ref, hl_ref, ct_ref,
         fcar, xbuf, prev_s, hcar) = refs
    else:
        (x_ref, g_ref, sh_ref, sc_ref, w_ref, bf_ref,
         xr_ref, gr_ref, ga_ref, gb_ref, lf_ref, q_ref, k_ref, v_ref) = refs
    x = x_ref[...]
    h = (_rms(x, g_ref[...]) * (1.0 + sc_ref[0]) + sh_ref[0]).astype(BF16)

    def seg(s, lhs=h):
        return jnp.dot(lhs, w_ref[:, s * D_MODEL:(s + 1) * D_MODEL], preferred_element_type=F32)

    if prompt:
        first = pl.program_id(0) % tiles_per_seq == 0

        @pl.when(first)
        def _():
            fcar[...] = jnp.zeros_like(fcar)
            prev_s[...] = jnp.zeros_like(prev_s)
            hcar[...] = jnp.zeros_like(hcar)

        hs = jnp.dot(perm_ref[...], h, preferred_element_type=F32).astype(BF16)
        xr = seg(0, hs)
        gr = seg(1, hs)
        n_rows = xr.shape[0]
        ct_ref[0] = jnp.concatenate(
            [xr[n_rows - 1 - i * SUBLANES:n_rows - i * SUBLANES, :]
             for i in reversed(range(CONV_W - 1))], axis=0)

        def rglru_chunk(c):
            lanes = slice(c * RG_CHUNK, (c + 1) * RG_CHUNK)
            ain_ref[:, lanes], hl_ref[0, :, lanes] = _rglru_tile(
                xr[:, lanes], gr[:, lanes], first, c, unperm_ref, cw_ref, cb_ref, wr_ref, wi_ref,
                br_ref, bi_ref, lam_ref, xbuf, prev_s, hcar)
        n_chunks = D_MODEL // RG_CHUNK
    else:
        xr_ref[...] = seg(0)
        gr_ref[...] = seg(1)
        n_chunks = 0

    half = D_MODEL // 2
    halves = {}
    for n, (s, i) in enumerate((s, i) for s in range(2, N_SEG) for i in range(2)):
        if n < n_chunks:
            rglru_chunk(n)
        cols = slice(s * D_MODEL + i * half, s * D_MODEL + (i + 1) * half)
        halves[s, i] = jnp.dot(h, w_ref[:, cols], preferred_element_type=F32)
    q, k, v, ga, gb = (jnp.concatenate([halves[s, 0], halves[s, 1]], axis=1)
                       for s in range(2, N_SEG))
    q = (q * (HEAD_DIM ** -0.5)).astype(BF16)
    ga_ref[...] = ga
    gb_ref[...] = gb
    fl = jnp.dot(h, w_ref[:, N_SEG * D_MODEL:], preferred_element_type=F32)
    lf = _log_sigmoid(fl + bf_ref[...])
    lf_ref[...] = lf
    if not prompt:
        q_ref[...] = q
        k_ref[...] = k
        v_ref[...] = v
        return

    tm = x.shape[0]
    kt_ref[0] = k.T
    vt = v.T
    vt_ref[0] = vt
    vtb_ref[0] = vt.astype(BF16)

    lane = lax.broadcasted_iota(jnp.int32, (1, LANES), 1)
    tri = tri_ref[...]
    cum = fcar[...]
    for piece in _split3(jnp.where(lane < N_HEADS, lf, 0.0)):
        cum = cum + jnp.dot(tri, piece.astype(BF16), preferred_element_type=F32)
    fcar[...] = cum[tm - 1:tm, :]

    c_hi, c_mid, c_lo = _split3(cum)
    faug = jnp.where(lane < N_HEADS, c_hi,
                     jnp.where(lane < 2 * N_HEADS, pltpu.roll(c_mid, N_HEADS, axis=1),
                               jnp.where(lane < ONES_LANE, pltpu.roll(c_lo, 2 * N_HEADS, axis=1),
                                         jnp.where(lane == ONES_LANE, 1.0, 0.0)))).astype(BF16)
    kb = k.astype(BF16)
    for j in range(N_HEADS // 2):
        pair = slice(j * LANES, (j + 1) * LANES)
        out = slice(j * MXU_DIM, (j + 1) * MXU_DIM)
        qa_ref[:, out] = jnp.dot(jnp.concatenate([q[:, pair], faug], axis=1), pq_ref[j],
                                 preferred_element_type=F32).astype(BF16)
        ka_ref[:, out] = jnp.dot(jnp.concatenate([kb[:, pair], faug], axis=1), pk_ref[j],
                                 preferred_element_type=F32).astype(BF16)


def _scan_order(tile):
    seg_len = tile // SUBLANES
    n = np.arange(tile)
    perm = np.zeros((tile, tile), np.float32)
    perm[n, (n % SUBLANES) * seg_len + n // SUBLANES] = 1.0
    return perm


def _placement_matrices():
    n_pairs = N_HEADS // 2
    pq = np.zeros((n_pairs, MXU_DIM, MXU_DIM), np.float32)
    pk = np.zeros((n_pairs, MXU_DIM, MXU_DIM), np.float32)
    for j in range(n_pairs):
        for s in range(2):
            head = 2 * j + s
            for d in range(HEAD_DIM):
                pq[j, s * HEAD_DIM + d, s * LANES + d] = 1.0
                pk[j, s * HEAD_DIM + d, s * LANES + d] = 1.0
            cb = s * LANES + AUG
            for i in range(N_SPLIT):
                pq[j, LANES + i * N_HEADS + head, cb + i] = 1.0
                pq[j, LANES + ONES_LANE, cb + N_SPLIT + i] = 1.0
                pk[j, LANES + ONES_LANE, cb + i] = 1.0
                pk[j, LANES + i * N_HEADS + head, cb + N_SPLIT + i] = -1.0
    return jnp.asarray(pq, BF16), jnp.asarray(pk, BF16)


def _in_proj(x, mod, rows_per_group, tm, norm_g, w_all, bf_pad, rg=None):
    prompt = rg is not None
    m = x.shape[0]
    r = mod.shape[1]
    row = lambda i: (i, 0)
    f32o = jax.ShapeDtypeStruct((m, D_MODEL), F32)
    bf16o = jax.ShapeDtypeStruct((m, D_MODEL), BF16)
    blk = pl.BlockSpec((tm, D_MODEL), row)
    in_specs = [blk, _resident((1, D_MODEL)),
                _mod_spec(rows_per_group, tm, r, 0), _mod_spec(rows_per_group, tm, r, 1),
                _resident((D_MODEL, W_ALL_COLS)), _resident((1, LANES))]
    operands = [x, norm_g, mod, mod, w_all, bf_pad]
    lf_spec = pl.BlockSpec((tm, LANES), row)
    lf_shape = jax.ShapeDtypeStruct((m, LANES), F32)
    scratch = []
    if prompt:
        tiles = rows_per_group // tm
        n_seq = m // rows_per_group
        halo = (CONV_W - 1) * SUBLANES
        tri = jnp.asarray(np.tril(np.ones((tm, tm), np.float32)), BF16)
        pq, pk = _placement_matrices()
        perm = _scan_order(tm)
        consts = [tri, pq, pk, jnp.asarray(perm, BF16), jnp.asarray(perm.T, BF16)]
        in_specs += [_resident(c.shape) for c in consts] + [_resident(p.shape) for p in rg]
        operands += consts + list(rg)
        wide = pl.BlockSpec((tm, 2 * D_MODEL), row)
        time_on_lanes = pl.BlockSpec((1, D_MODEL, tm), lambda i: (i // tiles, 0, i % tiles))
        per_seq = lambda rows: pl.BlockSpec((1, rows, D_MODEL), lambda i: (i // tiles, 0, 0))
        out_specs = ([blk, blk, lf_spec, wide, wide] + [time_on_lanes] * 3
                     + [blk, per_seq(1), per_seq(CONV_W - 1)])
        out_shape = ([f32o, f32o, lf_shape] + [jax.ShapeDtypeStruct((m, 2 * D_MODEL), BF16)] * 2
                     + [jax.ShapeDtypeStruct((n_seq, D_MODEL, rows_per_group), F32)] * 2
                     + [jax.ShapeDtypeStruct((n_seq, D_MODEL, rows_per_group), BF16), bf16o,
                        jax.ShapeDtypeStruct((n_seq, 1, D_MODEL), F32),
                        jax.ShapeDtypeStruct((n_seq, CONV_W - 1, D_MODEL), F32)])
        scratch = [pltpu.VMEM((1, LANES), F32), pltpu.VMEM((tm + halo, D_MODEL), F32),
                   pltpu.VMEM((halo, D_MODEL), F32), pltpu.VMEM((SUBLANES, D_MODEL), F32)]
    else:
        out_specs = [blk] * 4 + [lf_spec] + [blk] * 3
        out_shape = [f32o] * 4 + [lf_shape, bf16o, f32o, f32o]
    return pl.pallas_call(
        functools.partial(_in_proj_kernel, prompt=prompt, tiles_per_seq=rows_per_group // tm),
        grid=(m // tm,),
        in_specs=in_specs, out_specs=out_specs, out_shape=out_shape, scratch_shapes=scratch,
        compiler_params=_params("arbitrary"),
        name="in_proj_prompt" if prompt else "in_proj_sample",
    )(*operands)


def _block_gates(xc, wr_ref, wi_ref, br, bi):
    xcb = xc.astype(BF16)
    n = D_MODEL // RG_CHUNK
    pr = [jnp.dot(xcb[:, c * RG_CHUNK:(c + 1) * RG_CHUNK], wr_ref[c], preferred_element_type=F32)
          for c in range(n)]
    pi = [jnp.dot(xcb[:, c * RG_CHUNK:(c + 1) * RG_CHUNK], wi_ref[c], preferred_element_type=F32)
          for c in range(n)]
    gate_r = _sigmoid(jnp.concatenate(pr, axis=1) + br)
    gate_i = _sigmoid(jnp.concatenate(pi, axis=1) + bi)
    return gate_r, gate_i


def _decay_and_input(xc, gate_r, gate_i, lam, is_start):
    log_a = -RG_C * gate_r * _softplus(-lam)
    a = jnp.exp(log_a)
    mult = jnp.sqrt(jnp.tanh(-log_a) * (a * a + 1.0))
    if is_start is not None:
        mult = jnp.where(is_start, 1.0, mult)
    return a, mult * gate_i * xc


def _rglru_tile(x, gr, first, c, unperm_ref, cw_ref, cb_ref, wr_ref, wi_ref, br_ref, bi_ref,
                lam_ref, xbuf, prev_s, hcar):
    tt, width = x.shape
    lanes = slice(c * width, (c + 1) * width)
    seg_len = tt // SUBLANES
    halo = (CONV_W - 1) * SUBLANES
    sub = lax.broadcasted_iota(jnp.int32, (SUBLANES, 1), 0)
    for i in range(CONV_W - 1):
        rows = slice(i * SUBLANES, (i + 1) * SUBLANES)
        cur = x[tt - halo + i * SUBLANES:tt - halo + (i + 1) * SUBLANES, :]
        xbuf[rows, lanes] = jnp.where(sub == 0, pltpu.roll(prev_s[rows, lanes], 1, axis=0),
                                      pltpu.roll(cur, 1, axis=0))
    prev_s[:, lanes] = x[tt - halo:, :]
    xbuf[halo:, lanes] = x
    xc = cb_ref[:, lanes] + cw_ref[CONV_W - 1:CONV_W, lanes] * x
    for j in range(CONV_W - 1):
        back = CONV_W - 1 - j
        xc = xc + cw_ref[j:j + 1, lanes] * xbuf[pl.ds(halo - back * SUBLANES, tt), lanes]

    xcb = xc.astype(BF16)
    gate_r = _sigmoid(jnp.dot(xcb, wr_ref[c], preferred_element_type=F32) + br_ref[:, lanes])
    gate_i = _sigmoid(jnp.dot(xcb, wi_ref[c], preferred_element_type=F32) + bi_ref[:, lanes])
    is_start = (lax.broadcasted_iota(jnp.int32, (tt, 1), 0) + jnp.where(first, 0, 1)) == 0
    a, u = _decay_and_input(xc, gate_r, gate_i, lam_ref[:, lanes], is_start)

    hh = jnp.zeros((SUBLANES, width), F32)
    pp = jnp.ones((SUBLANES, width), F32)
    local, decay = [], []
    for g in range(seg_len):
        rows = slice(g * SUBLANES, (g + 1) * SUBLANES)
        hh = a[rows, :] * hh + u[rows, :]
        pp = a[rows, :] * pp
        local.append(hh)
        decay.append(pp)
    cy = hcar[0:1, lanes]
    carries = []
    for s in range(SUBLANES):
        carries.append(cy)
        cy = hh[s:s + 1, :] + pp[s:s + 1, :] * cy
    hcar[0:1, lanes] = cy
    cin = jnp.concatenate(carries, axis=0)
    hr = jnp.concatenate([local[g] + decay[g] * cin for g in range(seg_len)], axis=0)
    gated = (hr * jax.nn.gelu(gr)).astype(BF16)
    return jnp.dot(unperm_ref[...], gated, preferred_element_type=F32).astype(BF16), cy


def _rglru_step_kernel(xr_ref, gr_ref, st_ref, h0_ref, cw_ref, cb_ref, wr_ref, wi_ref, br_ref,
                       bi_ref, lam_ref, o_ref, h_ref, *, at_start):
    xr = xr_ref[...]
    xc = cb_ref[...] + cw_ref[CONV_W - 1:CONV_W, :] * xr
    for j in range(CONV_W - 1):
        xc = xc + cw_ref[j:j + 1, :] * st_ref[j]
    gate_r, gate_i = _block_gates(xc, wr_ref, wi_ref, br_ref[...], bi_ref[...])
    a, u = _decay_and_input(xc, gate_r, gate_i, lam_ref[...], True if at_start else None)
    h = a * h0_ref[...] + u
    h_ref[...] = h
    o_ref[...] = (h * jax.nn.gelu(gr_ref[...])).astype(BF16)


def _rglru_step(xr, gr, state_t, h0, rg, at_start):
    n = xr.shape[0]
    full = lambda shape: pl.BlockSpec(shape, lambda i: (0,) * len(shape))
    nchunk = D_MODEL // RG_CHUNK
    return pl.pallas_call(
        functools.partial(_rglru_step_kernel, at_start=at_start),
        grid=(1,),
        in_specs=[full((n, D_MODEL)), full((n, D_MODEL)), full((CONV_W - 1, n, D_MODEL)),
                  full((n, D_MODEL)), full((CONV_W, D_MODEL)), full((1, D_MODEL)),
                  full((nchunk, RG_CHUNK, RG_CHUNK)), full((nchunk, RG_CHUNK, RG_CHUNK)),
                  full((1, D_MODEL)), full((1, D_MODEL)), full((1, D_MODEL))],
        out_specs=[full((n, D_MODEL)), full((n, D_MODEL))],
        out_shape=[jax.ShapeDtypeStruct((n, D_MODEL), BF16),
                   jax.ShapeDtypeStruct((n, D_MODEL), F32)],
        compiler_params=_params("arbitrary"),
        name="rglru_sample",
    )(xr, gr, state_t, h0, *rg)


def _prompt_update(q_ref, k_ref, vt_ref, m_s, l_s, acc_s, diagonal, tq, tk):
    for h in range(ATT_HEADS):
        lanes = slice(h * LANES, (h + 1) * LANES)
        st = lax.dot_general(k_ref[:, lanes], q_ref[:, lanes], (((1,), (1,)), ((), ())),
                             preferred_element_type=F32)
        if diagonal:
            kpos = lax.broadcasted_iota(jnp.int32, (tk, tq), 0)
            qpos = lax.broadcasted_iota(jnp.int32, (tk, tq), 1)
            st = jnp.where(kpos <= qpos, st, -jnp.inf)
        m_prev = m_s[h]
        m_new = jnp.maximum(m_prev, jnp.max(st, axis=0, keepdims=True))
        alpha = jnp.exp(m_prev - m_new)
        p = jnp.exp(st - m_new)
        l_s[h] = alpha * l_s[h] + jnp.sum(p, axis=0, keepdims=True)
        pv = jnp.dot(vt_ref[0, h * HEAD_DIM:(h + 1) * HEAD_DIM, :], p.astype(BF16),
                     preferred_element_type=F32)
        acc_s[h] = alpha * acc_s[h] + pv
        m_s[h] = m_new


def _paged_update(valid, k_refs, v_refs, lf_refs, qb_s, m_s, l_s, acc_s, car_s):
    g = len(k_refs)
    lane = lax.broadcasted_iota(jnp.int32, (N_HEADS, PAGE_SIZE), 1)

    carry_in = car_s[...]
    carry = carry_in
    bias = []
    for j in range(g):
        pref = lf_refs[j][0, 0]
        d = 1
        while d < PAGE_SIZE:
            pref = pref + jnp.where(lane >= d, pltpu.roll(pref, d, axis=1), 0.0)
            d *= 2
        bias.append(carry + pref)
        carry = carry + pref[:, PAGE_SIZE - 1:PAGE_SIZE]
    car_s[...] = jnp.where(valid, carry, carry_in)

    for hg in range(N_HEADS // SUBLANES):
        heads = range(hg * SUBLANES, (hg + 1) * SUBLANES)
        grp = slice(hg * SUBLANES, (hg + 1) * SUBLANES)
        s = jnp.concatenate(
            [jnp.concatenate([jnp.sum(k_refs[j][0, 0, h] * qb_s[h], axis=0, keepdims=True)
                              for h in heads], axis=0) - bias[j][grp, :]
             for j in range(g)], axis=1)
        m_prev = m_s[grp, :]
        m_grp = jnp.where(valid, jnp.maximum(m_prev, jnp.max(s, axis=1, keepdims=True)), m_prev)
        alpha = jnp.where(valid, jnp.exp(m_prev - m_grp), 1.0)
        pr = jnp.where(valid, jnp.exp(s - m_grp), 0.0)
        l_s[grp, :] = alpha * l_s[grp, :] + jnp.sum(pr, axis=1, keepdims=True)
        m_s[grp, :] = m_grp
        for i, h in enumerate(heads):
            a_h = acc_s[h] * alpha[i:i + 1, :]
            for j in range(g):
                a_h = a_h + pr[i:i + 1, j * PAGE_SIZE:(j + 1) * PAGE_SIZE] * v_refs[j][0, 0, h]
            acc_s[h] = a_h


def _paged_finish(qr_ref, kc_ref, vc_ref, lc_ref, o_ref, m_s, l_s, acc_s, car_s):
    m_past = m_s[...]
    s_c = jnp.sum(qr_ref[0] * kc_ref[0], axis=1, keepdims=True) - (car_s[...] + lc_ref[0])
    m_f = jnp.maximum(m_past, s_c)
    al = jnp.exp(m_past - m_f)
    p_c = jnp.exp(s_c - m_f)
    inv_l = 1.0 / (al * l_s[...] + p_c)
    for h in range(N_HEADS):
        past = jnp.sum(acc_s[h], axis=1, keepdims=True) * al[h:h + 1, :]
        o_ref[0, h] = (past + p_c[h:h + 1, :] * vc_ref[0, h]) * inv_l[h:h + 1, :]


def _fox_kernel(qt_ref, kt_ref, pt_ref, q_ref, k_ref, vt_ref, qr_ref, qc_ref, kc_ref, vc_ref,
                lc_ref, *rest, tq, tk, g, n_paged_steps, steps_per_seq):
    k_refs = rest[0:g]
    v_refs = rest[g:2 * g]
    lf_refs = rest[2 * g:3 * g]
    o_ref, os_ref = rest[3 * g], rest[3 * g + 1]
    m_s, l_s, acc_s, qb_s, pm_s, pl_s, pacc_s, car_s = rest[3 * g + 2:]
    t = pl.program_id(2)
    qi = qt_ref[t]
    ki = kt_ref[t]
    step = (pl.program_id(0) * pl.num_programs(1) + pl.program_id(1)) * pl.num_programs(2) + t
    valid = step < n_paged_steps
    page_step = jnp.minimum(step, n_paged_steps - 1) % steps_per_seq

    @pl.when(ki == 0)
    def _():
        m_s[...] = jnp.full_like(m_s, -jnp.inf)
        l_s[...] = jnp.zeros_like(l_s)
        acc_s[...] = jnp.zeros_like(acc_s)

    @pl.when(jnp.logical_and(valid, page_step == 0))
    def _():
        qb_s[...] = jnp.broadcast_to(qc_ref[0], qb_s.shape)
        pm_s[...] = jnp.full_like(pm_s, -jnp.inf)
        pl_s[...] = jnp.zeros_like(pl_s)
        pacc_s[...] = jnp.zeros_like(pacc_s)
        car_s[...] = jnp.zeros_like(car_s)

    def both(diagonal):
        _prompt_update(q_ref, k_ref, vt_ref, m_s, l_s, acc_s, diagonal, tq, tk)
        _paged_update(valid, k_refs, v_refs, lf_refs, qb_s, pm_s, pl_s, pacc_s, car_s)

    @pl.when(ki < qi)
    def _():
        both(False)

    @pl.when(ki == qi)
    def _():
        both(True)
        ot = jnp.concatenate([acc_s[h] / l_s[h] for h in range(ATT_HEADS)], axis=0)
        o_ref[...] = ot.T.astype(BF16)

    @pl.when(jnp.logical_and(valid, page_step == steps_per_seq - 1))
    def _():
        _paged_finish(qr_ref, kc_ref, vc_ref, lc_ref, os_ref, pm_s, pl_s, pacc_s, car_s)


def _fox_attention(q_aug, k_aug, vt, n_batch, seq, q, k_cur, v_cur, lf_cur, cache_kt, cache_vt,
                   cache_lft, page_table):
    tq = tk = ATT_TILE
    nq = seq // tq
    pairs = [(i, j) for i in range(nq) for j in range(i + 1)]
    qt = jnp.asarray(np.array([p[0] for p in pairs], np.int32))
    kt = jnp.asarray(np.array([p[1] for p in pairs], np.int32))
    groups = N_HEADS // ATT_HEADS
    wide = ATT_HEADS * LANES
    narrow = ATT_HEADS * HEAD_DIM

    n, n_pages = page_table.shape
    g = PAGES_PER_STEP
    steps_per_seq = n_pages // g
    n_paged_steps = n * steps_per_seq
    assert n_paged_steps <= n_batch * groups * len(pairs), "not enough grid steps for the cache pages"

    def seq_and_step(b, gi, t):
        s = jnp.minimum((b * groups + gi) * len(pairs) + t, n_paged_steps - 1)
        return s // steps_per_seq, s % steps_per_seq

    def per_seq(shape):
        zeros = (0,) * len(shape)
        return pl.BlockSpec((1,) + shape,
                            lambda b, gi, t, qt, kt, pt: (seq_and_step(b, gi, t)[0],) + zeros)

    def page(tail, j):
        zeros = (0,) * len(tail)

        def index(b, gi, t, qt, kt, pt):
            sq, st = seq_and_step(b, gi, t)
            return (0, pt[sq, st * g + j]) + zeros
        return pl.BlockSpec((1, 1) + tail, index)

    row, col = per_seq((N_HEADS, HEAD_DIM)), per_seq((N_HEADS, HEAD_DIM, 1))
    grid_spec = pltpu.PrefetchScalarGridSpec(
        num_scalar_prefetch=3,
        grid=(n_batch, groups, len(pairs)),
        in_specs=([pl.BlockSpec((tq, wide), lambda b, gi, t, qt, kt, pt: (b * nq + qt[t], gi)),
                   pl.BlockSpec((tk, wide), lambda b, gi, t, qt, kt, pt: (b * nq + kt[t], gi)),
                   pl.BlockSpec((1, narrow, tk), lambda b, gi, t, qt, kt, pt: (b, gi, kt[t])),
                   row, col, row, col, per_seq((N_HEADS, 1))]
                  + [page((N_HEADS, HEAD_DIM, PAGE_SIZE), j) for j in range(g)]
                  + [page((N_HEADS, HEAD_DIM, PAGE_SIZE), j) for j in range(g)]
                  + [page((N_HEADS, PAGE_SIZE), j) for j in range(g)]),
        out_specs=[pl.BlockSpec((tq, narrow), lambda b, gi, t, qt, kt, pt: (b * nq + qt[t], gi)),
                   col],
        scratch_shapes=[pltpu.VMEM((ATT_HEADS, 1, tq), F32), pltpu.VMEM((ATT_HEADS, 1, tq), F32),
                        pltpu.VMEM((ATT_HEADS, HEAD_DIM, tq), F32),
                        pltpu.VMEM((N_HEADS, HEAD_DIM, PAGE_SIZE), F32),
                        pltpu.VMEM((N_HEADS, 1), F32), pltpu.VMEM((N_HEADS, 1), F32),
                        pltpu.VMEM((N_HEADS, HEAD_DIM, PAGE_SIZE), F32),
                        pltpu.VMEM((N_HEADS, 1), F32)],
    )
    return pl.pallas_call(
        functools.partial(_fox_kernel, tq=tq, tk=tk, g=g, n_paged_steps=n_paged_steps,
                          steps_per_seq=steps_per_seq),
        grid_spec=grid_spec,
        out_shape=[jax.ShapeDtypeStruct((n_batch * seq, D_MODEL), BF16),
                   jax.ShapeDtypeStruct((n, N_HEADS, HEAD_DIM, 1), F32)],
        compiler_params=_params("arbitrary", "arbitrary", "arbitrary"),
        name="fox_attention",
    )(qt, kt, page_table, q_aug, k_aug, vt, q, q[..., None], k_cur, v_cur[..., None], lf_cur,
      *([cache_kt] * g), *([cache_vt] * g), *([cache_lft] * g))


def _merge_kernel(x_ref, a_ref, b_ref, ga_ref, gb_ref, g1_ref, sh_ref, sc_ref, n2_ref,
                  wa_ref, wb_ref, wo_ref, x1_ref, h2_ref):
    ya = jnp.dot(a_ref[...], wa_ref[...], preferred_element_type=F32)
    yb = jnp.dot(b_ref[...], wb_ref[...], preferred_element_type=F32)
    mix = (_sigmoid(ga_ref[...]) * ya + _sigmoid(gb_ref[...]) * yb).astype(BF16)
    x1 = x_ref[...] + g1_ref[0] * jnp.dot(mix, wo_ref[...], preferred_element_type=F32)
    x1_ref[...] = x1
    h2_ref[...] = (_rms(x1, n2_ref[...]) * (1.0 + sc_ref[0]) + sh_ref[0]).astype(BF16)


def _merge(x, a_in, b_in, ga, gb, mod, rows_per_group, tm, norm_g, wa, wb, wo):
    m = x.shape[0]
    r = mod.shape[1]
    blk = pl.BlockSpec((tm, D_MODEL), lambda i: (i, 0))
    sq = _resident((D_MODEL, D_MODEL))
    return pl.pallas_call(
        _merge_kernel,
        grid=(m // tm,),
        in_specs=[blk, blk, blk, blk, blk,
                  _mod_spec(rows_per_group, tm, r, 2), _mod_spec(rows_per_group, tm, r, 3),
                  _mod_spec(rows_per_group, tm, r, 4), _resident((1, D_MODEL)), sq, sq, sq],
        out_specs=[blk, blk],
        out_shape=[jax.ShapeDtypeStruct((m, D_MODEL), F32), jax.ShapeDtypeStruct((m, D_MODEL), BF16)],
        compiler_params=_params("arbitrary"),
        name="merge_proj",
    )(x, a_in, b_in, ga, gb, mod, mod, mod, norm_g, wa, wb, wo)


def _ffn_kernel(x1_ref, h2_ref, g2_ref, nf_ref, wi_ref, wo_ref, y_ref):
    h2 = h2_ref[...]
    gu = jnp.dot(h2, wi_ref[:, 0:D_FF], preferred_element_type=F32)
    up = jnp.dot(h2, wi_ref[:, D_FF:2 * D_FF], preferred_element_type=F32)
    act = (gu * _sigmoid(gu) * up).astype(BF16)
    x2 = x1_ref[...] + g2_ref[0] * jnp.dot(act, wo_ref[...], preferred_element_type=F32)
    y_ref[...] = _rms(x2, nf_ref[...])


def _ffn(x1, h2, mod, rows_per_group, tm, normf_g, w_in, w_out):
    m = x1.shape[0]
    r = mod.shape[1]
    blk = pl.BlockSpec((tm, D_MODEL), lambda i: (i, 0))
    return pl.pallas_call(
        _ffn_kernel,
        grid=(m // tm,),
        in_specs=[blk, blk, _mod_spec(rows_per_group, tm, r, 5), _resident((1, D_MODEL)),
                  _resident((D_MODEL, 2 * D_FF)), _resident((D_FF, D_MODEL))],
        out_specs=blk,
        out_shape=jax.ShapeDtypeStruct((m, D_MODEL), F32),
        compiler_params=_params("arbitrary"),
        name="ffn_final",
    )(x1, h2, mod, normf_g, w_in, w_out)


def _pair_blocks(w):
    per = RG_CHUNK // (D_MODEL // N_BLK)
    bw = D_MODEL // N_BLK
    w = w.reshape(N_BLK // per, per, bw, bw)
    eye = jnp.eye(per, dtype=w.dtype)
    return jnp.einsum("cpij,pq->cpiqj", w, eye).reshape(N_BLK // per, RG_CHUNK, RG_CHUNK).astype(BF16)


def kernel(x_prompt, x_sample, c_prompt, c_sample, cache_k, cache_v, cache_logf, state_conv,
           state_rglru, page_table, ada_w, ada_b, norm1_g, norm2_g, normf_g, w_in, b_f, conv_w,
           conv_b, rg_wr, rg_br, rg_wi, rg_bi, rg_lambda, w_proj_a, w_proj_b, w_o, w_ffn_in,
           w_ffn_out):
    depth = ada_w.shape[0]
    assert depth == 1, "single-layer trunk"
    bsz, seq, d = x_prompt.shape
    nsmp = x_sample.shape[0]
    assert x_sample.shape[1] == 1 and d == D_MODEL
    past_len = page_table.shape[1] * PAGE_SIZE
    attn_w = N_HEADS * HEAD_DIM
    hd = (N_HEADS, HEAD_DIM)

    wl = w_in[0]
    cut = 2 * D_MODEL + 3 * attn_w
    w_all = jnp.concatenate(
        [wl[:, :cut], wl[:, cut + N_HEADS:], jnp.pad(wl[:, cut:cut + N_HEADS], ((0, 0), (0, LANES - N_HEADS)))],
        axis=1).astype(BF16)
    bf_pad = jnp.pad(b_f[0], (0, LANES - N_HEADS)).reshape(1, LANES)
    rg = (conv_w[0], conv_b[0].reshape(1, d), _pair_blocks(rg_wr[0]), _pair_blocks(rg_wi[0]),
          rg_br[0].reshape(1, d), rg_bi[0].reshape(1, d), rg_lambda[0].reshape(1, d))
    wa, wb, wo = (w[0].astype(BF16) for w in (w_proj_a, w_proj_b, w_o))
    wfi, wfo = w_ffn_in[0].astype(BF16), w_ffn_out[0].astype(BF16)
    n1, n2, nf = norm1_g[0].reshape(1, d), norm2_g[0].reshape(1, d), normf_g.reshape(1, d)

    mod = _ada(jnp.concatenate([c_prompt, c_sample], axis=0), ada_w[0], ada_b[0])
    mod_p = mod[:bsz].reshape(bsz, 1, 6 * d)
    mod_s = mod[bsz:].reshape(1, nsmp, 6 * d)

    xp = x_prompt.reshape(bsz * seq, d)
    ga, gb, lf, q_aug, k_aug, kt, vt, vtb, a_in, h_last_p, conv_p = _in_proj(
        xp, mod_p, seq, ROW_TILE, n1, w_all, bf_pad, rg)
    logf_p = lf[:, :N_HEADS].reshape(bsz, seq, N_HEADS)
    xs = x_sample.reshape(nsmp, d)
    xr_s, gr_s, ga_s, gb_s, lf_s, q_s, k_s, v_s = _in_proj(xs, mod_s, nsmp, nsmp, n1, w_all, bf_pad)
    logf_s = lf_s[:, :N_HEADS]
    a_in_s, h_s = _rglru_step(xr_s, gr_s, jnp.transpose(state_conv[0], (1, 0, 2)), state_rglru[0],
                              rg, past_len == 0)

    att, att_s = _fox_attention(
        q_aug, k_aug, vtb, bsz, seq,
        q_s.astype(F32).reshape(nsmp, *hd), k_s.reshape(nsmp, *hd), v_s.reshape(nsmp, *hd),
        logf_s.reshape(nsmp, N_HEADS, 1),
        jnp.transpose(cache_k, (0, 1, 3, 4, 2)), jnp.transpose(cache_v, (0, 1, 3, 4, 2)),
        jnp.transpose(cache_logf, (0, 1, 3, 2)), page_table)

    k_p = jnp.transpose(kt.reshape(1, bsz, *hd, seq), (0, 1, 4, 2, 3))
    v_p = jnp.transpose(vt.reshape(1, bsz, *hd, seq), (0, 1, 4, 2, 3))
    x1, h2 = _merge(xp, a_in, att, ga, gb, mod_p, seq, DENSE_TILE, n2, wa, wb, wo)
    y_p = _ffn(x1, h2, mod_p, seq, DENSE_TILE, nf, wfi, wfo).reshape(bsz, seq, d)

    x1_s, h2_s = _merge(xs, a_in_s, att_s.reshape(nsmp, d).astype(BF16), ga_s, gb_s, mod_s, nsmp,
                        nsmp, n2, wa, wb, wo)
    y_s = _ffn(x1_s, h2_s, mod_s, nsmp, nsmp, nf, wfi, wfo).reshape(nsmp, 1, d)
    conv_s = jnp.concatenate([state_conv[0][:, 1:], xr_s[:, None, :]], axis=1)

    return (y_p, y_s,
            k_p, v_p, logf_p[None],
            conv_p[None], h_last_p.reshape(1, bsz, d),
            k_s.reshape(1, nsmp, 1, *hd), v_s.reshape(1, nsmp, 1, *hd),
            logf_s.reshape(1, nsmp, 1, N_HEADS),
            conv_s[None], h_s[None])
```

```python
import functools

import numpy as np
import jax
import jax.numpy as jnp
from jax import lax
from jax.experimental import pallas as pl
from jax.experimental.pallas import tpu as pltpu

F32 = jnp.float32
BF16 = jnp.bfloat16

D_MODEL = 1024
N_HEADS = 16
HEAD_DIM = 64
N_BLK = 16
CONV_W = 4
RG_C = 8.0
PAGE_SIZE = 128
D_FF = 2816
EPS = 1e-6

LANES = 128
SUBLANES = 8
MXU_DIM = 256
VMEM_LIMIT = 56 * 1024 * 1024

ROW_TILE = 256
DENSE_TILE = 512
ATT_TILE = 512
ATT_HEADS = 8
PAGES_PER_STEP = 8

RG_CHUNK = 128
N_SEG = 7
W_ALL_COLS = N_SEG * D_MODEL + LANES

AUG = HEAD_DIM
N_SPLIT = 3
ONES_LANE = N_SPLIT * N_HEADS


def _params(*sem):
    return pltpu.CompilerParams(dimension_semantics=sem, vmem_limit_bytes=VMEM_LIMIT)


def _resident(shape):
    nd = len(shape)
    return pl.BlockSpec(shape, lambda *_: (0,) * nd, pipeline_mode=pl.Buffered(1))


def _sigmoid(x):
    return 1.0 / (1.0 + jnp.exp(-x))


def _log_sigmoid(x):
    return jnp.minimum(x, 0.0) - jnp.log1p(jnp.exp(-jnp.abs(x)))


def _softplus(x):
    return jnp.maximum(x, 0.0) + jnp.log1p(jnp.exp(-jnp.abs(x)))


def _rms(x, g):
    return x * lax.rsqrt(jnp.mean(x * x, axis=-1, keepdims=True) + EPS) * g


def _ada_kernel(c_ref, w_ref, b_ref, o_ref):
    c = c_ref[...]
    s = (c * _sigmoid(c)).astype(BF16)
    o_ref[...] = jnp.dot(s, w_ref[...].astype(BF16), preferred_element_type=F32) + b_ref[...]


def _ada(c, w, b):
    n, d = c.shape
    cols = w.shape[1]
    return pl.pallas_call(
        _ada_kernel,
        grid=(cols // d,),
        in_specs=[pl.BlockSpec((n, d), lambda j: (0, 0)),
                  pl.BlockSpec((d, d), lambda j: (0, j)),
                  pl.BlockSpec((1, d), lambda j: (0, j))],
        out_specs=pl.BlockSpec((n, d), lambda j: (0, j)),
        out_shape=jax.ShapeDtypeStruct((n, cols), F32),
        compiler_params=_params("arbitrary"),
        name="ada_modulation",
    )(c, w, b.reshape(1, cols))


def _mod_spec(rows_per_group, tm, r, chunk):
    tiles = rows_per_group // tm
    return pl.BlockSpec((1, r, D_MODEL), lambda i: (i // tiles, 0, chunk))


def _split3(x):
    hi = x.astype(BF16).astype(F32)
    r = x - hi
    mid = r.astype(BF16).astype(F32)
    return hi, mid, r - mid


def _in_proj_kernel(*refs, prompt, tiles_per_seq):
    if prompt:
        (x_ref, g_ref, sh_ref, sc_ref, w_ref, bf_ref, tri_ref, pq_ref, pk_ref, perm_ref, unperm_ref,
         cw_ref, cb_ref, wr_ref, wi_ref, br_ref, bi_ref, lam_ref,
         ga_ref, gb_ref, lf_ref, qa_ref, ka_ref, kt_ref, vt_ref, vtb_ref, ain_ref, hl_ref, ct_ref,
         fcar, xbuf, prev_s, hcar) = refs
    else:
        (x_ref, g_ref, sh_ref, sc_ref, w_ref, bf_ref,
         xr_ref, gr_ref, ga_ref, gb_ref, lf_ref, q_ref, k_ref, v_ref) = refs
    x = x_ref[...]
    h = (_rms(x, g_ref[...]) * (1.0 + sc_ref[0]) + sh_ref[0]).astype(BF16)

    def seg(s, lhs=h):
        return jnp.dot(lhs, w_ref[:, s * D_MODEL:(s + 1) * D_MODEL], preferred_element_type=F32)

    if prompt:
        first = pl.program_id(0) % tiles_per_seq == 0

        @pl.when(first)
        def _():
            fcar[...] = jnp.zeros_like(fcar)
            prev_s[...] = jnp.zeros_like(prev_s)
            hcar[...] = jnp.zeros_like(hcar)

        hs = jnp.dot(perm_ref[...], h, preferred_element_type=F32).astype(BF16)
        xr = seg(0, hs)
        gr = seg(1, hs)
        n_rows = xr.shape[0]
        ct_ref[0] = jnp.concatenate(
            [xr[n_rows - 1 - i * SUBLANES:n_rows - i * SUBLANES, :]
             for i in reversed(range(CONV_W - 1))], axis=0)

        def rglru_chunk(c):
            lanes = slice(c * RG_CHUNK, (c + 1) * RG_CHUNK)
            ain_ref[:, lanes], hl_ref[0, :, lanes] = _rglru_tile(
                xr[:, lanes], gr[:, lanes], first, c, unperm_ref, cw_ref, cb_ref, wr_ref, wi_ref,
                br_ref, bi_ref, lam_ref, xbuf, prev_s, hcar)
        n_chunks = D_MODEL // RG_CHUNK
    else:
        xr_ref[...] = seg(0)
        gr_ref[...] = seg(1)
        n_chunks = 0

    half = D_MODEL // 2
    halves = {}
    for n, (s, i) in enumerate((s, i) for s in range(2, N_SEG) for i in range(2)):
        if n < n_chunks:
            rglru_chunk(n)
        cols = slice(s * D_MODEL + i * half, s * D_MODEL + (i + 1) * half)
        halves[s, i] = jnp.dot(h, w_ref[:, cols], preferred_element_type=F32)
    q, k, v, ga, gb = (jnp.concatenate([halves[s, 0], halves[s, 1]], axis=1)
                       for s in range(2, N_SEG))
    q = (q * (HEAD_DIM ** -0.5)).astype(BF16)
    ga_ref[...] = ga
    gb_ref[...] = gb
    fl = jnp.dot(h, w_ref[:, N_SEG * D_MODEL:], preferred_element_type=F32)
    lf = _log_sigmoid(fl + bf_ref[...])
    lf_ref[...] = lf
    if not prompt:
        q_ref[...] = q
        k_ref[...] = k
        v_ref[...] = v
        return

    tm = x.shape[0]
    kt_ref[0] = k.T
    vt = v.T
    vt_ref[0] = vt
    vtb_ref[0] = vt.astype(BF16)

    lane = lax.broadcasted_iota(jnp.int32, (1, LANES), 1)
    tri = tri_ref[...]
    cum = fcar[...]
    for piece in _split3(jnp.where(lane < N_HEADS, lf, 0.0)):
        cum = cum + jnp.dot(tri, piece.astype(BF16), preferred_element_type=F32)
    fcar[...] = cum[tm - 1:tm, :]

    c_hi, c_mid, c_lo = _split3(cum)
    faug = jnp.where(lane < N_HEADS, c_hi,
                     jnp.where(lane < 2 * N_HEADS, pltpu.roll(c_mid, N_HEADS, axis=1),
                               jnp.where(lane < ONES_LANE, pltpu.roll(c_lo, 2 * N_HEADS, axis=1),
                                         jnp.where(lane == ONES_LANE, 1.0, 0.0)))).astype(BF16)
    kb = k.astype(BF16)
    for j in range(N_HEADS // 2):
        pair = slice(j * LANES, (j + 1) * LANES)
        out = slice(j * MXU_DIM, (j + 1) * MXU_DIM)
        qa_ref[:, out] = jnp.dot(jnp.concatenate([q[:, pair], faug], axis=1), pq_ref[j],
                                 preferred_element_type=F32).astype(BF16)
        ka_ref[:, out] = jnp.dot(jnp.concatenate([kb[:, pair], faug], axis=1), pk_ref[j],
                                 preferred_element_type=F32).astype(BF16)


def _scan_order(tile):
    seg_len = tile // SUBLANES
    n = np.arange(tile)
    perm = np.zeros((tile, tile), np.float32)
    perm[n, (n % SUBLANES) * seg_len + n // SUBLANES] = 1.0
    return perm


def _placement_matrices():
    n_pairs = N_HEADS // 2
    pq = np.zeros((n_pairs, MXU_DIM, MXU_DIM), np.float32)
    pk = np.zeros((n_pairs, MXU_DIM, MXU_DIM), np.float32)
    for j in range(n_pairs):
        for s in range(2):
            head = 2 * j + s
            for d in range(HEAD_DIM):
                pq[j, s * HEAD_DIM + d, s * LANES + d] = 1.0
                pk[j, s * HEAD_DIM + d, s * LANES + d] = 1.0
            cb = s * LANES + AUG
            for i in range(N_SPLIT):
                pq[j, LANES + i * N_HEADS + head, cb + i] = 1.0
                pq[j, LANES + ONES_LANE, cb + N_SPLIT + i] = 1.0
                pk[j, LANES + ONES_LANE, cb + i] = 1.0
                pk[j, LANES + i * N_HEADS + head, cb + N_SPLIT + i] = -1.0
    return jnp.asarray(pq, BF16), jnp.asarray(pk, BF16)


def _in_proj(x, mod, rows_per_group, tm, norm_g, w_all, bf_pad, rg=None):
    prompt = rg is not None
    m = x.shape[0]
    r = mod.shape[1]
    row = lambda i: (i, 0)
    f32o = jax.ShapeDtypeStruct((m, D_MODEL), F32)
    bf16o = jax.ShapeDtypeStruct((m, D_MODEL), BF16)
    blk = pl.BlockSpec((tm, D_MODEL), row)
    in_specs = [blk, _resident((1, D_MODEL)),
                _mod_spec(rows_per_group, tm, r, 0), _mod_spec(rows_per_group, tm, r, 1),
                _resident((D_MODEL, W_ALL_COLS)), _resident((1, LANES))]
    operands = [x, norm_g, mod, mod, w_all, bf_pad]
    lf_spec = pl.BlockSpec((tm, LANES), row)
    lf_shape = jax.ShapeDtypeStruct((m, LANES), F32)
    scratch = []
    if prompt:
        tiles = rows_per_group // tm
        n_seq = m // rows_per_group
        halo = (CONV_W - 1) * SUBLANES
        tri = jnp.asarray(np.tril(np.ones((tm, tm), np.float32)), BF16)
        pq, pk = _placement_matrices()
        perm = _scan_order(tm)
        consts = [tri, pq, pk, jnp.asarray(perm, BF16), jnp.asarray(perm.T, BF16)]
        in_specs += [_resident(c.shape) for c in consts] + [_resident(p.shape) for p in rg]
        operands += consts + list(rg)
        wide = pl.BlockSpec((tm, 2 * D_MODEL), row)
        time_on_lanes = pl.BlockSpec((1, D_MODEL, tm), lambda i: (i // tiles, 0, i % tiles))
        per_seq = lambda rows: pl.BlockSpec((1, rows, D_MODEL), lambda i: (i // tiles, 0, 0))
        out_specs = ([blk, blk, lf_spec, wide, wide] + [time_on_lanes] * 3
                     + [blk, per_seq(1), per_seq(CONV_W - 1)])
        out_shape = ([f32o, f32o, lf_shape] + [jax.ShapeDtypeStruct((m, 2 * D_MODEL), BF16)] * 2
                     + [jax.ShapeDtypeStruct((n_seq, D_MODEL, rows_per_group), F32)] * 2
                     + [jax.ShapeDtypeStruct((n_seq, D_MODEL, rows_per_group), BF16), bf16o,
                        jax.ShapeDtypeStruct((n_seq, 1, D_MODEL), F32),
                        jax.ShapeDtypeStruct((n_seq, CONV_W - 1, D_MODEL), F32)])
        scratch = [pltpu.VMEM((1, LANES), F32), pltpu.VMEM((tm + halo, D_MODEL), F32),
                   pltpu.VMEM((halo, D_MODEL), F32), pltpu.VMEM((SUBLANES, D_MODEL), F32)]
    else:
        out_specs = [blk] * 4 + [lf_spec] + [blk] * 3
        out_shape = [f32o] * 4 + [lf_shape, bf16o, f32o, f32o]
    return pl.pallas_call(
        functools.partial(_in_proj_kernel, prompt=prompt, tiles_per_seq=rows_per_group // tm),
        grid=(m // tm,),
        in_specs=in_specs, out_specs=out_specs, out_shape=out_shape, scratch_shapes=scratch,
        compiler_params=_params("arbitrary"),
        name="in_proj_prompt" if prompt else "in_proj_sample",
    )(*operands)


def _block_gates(xc, wr_ref, wi_ref, br, bi):
    xcb = xc.astype(BF16)
    n = D_MODEL // RG_CHUNK
    pr = [jnp.dot(xcb[:, c * RG_CHUNK:(c + 1) * RG_CHUNK], wr_ref[c], preferred_element_type=F32)
          for c in range(n)]
    pi = [jnp.dot(xcb[:, c * RG_CHUNK:(c + 1) * RG_CHUNK], wi_ref[c], preferred_element_type=F32)
          for c in range(n)]
    gate_r = _sigmoid(jnp.concatenate(pr, axis=1) + br)
    gate_i = _sigmoid(jnp.concatenate(pi, axis=1) + bi)
    return gate_r, gate_i


def _decay_and_input(xc, gate_r, gate_i, lam, is_start):
    log_a = -RG_C * gate_r * _softplus(-lam)
    a = jnp.exp(log_a)
    mult = jnp.sqrt(jnp.tanh(-log_a) * (a * a + 1.0))
    if is_start is not None:
        mult = jnp.where(is_start, 1.0, mult)
    return a, mult * gate_i * xc


def _rglru_tile(x, gr, first, c, unperm_ref, cw_ref, cb_ref, wr_ref, wi_ref, br_ref, bi_ref,
                lam_ref, xbuf, prev_s, hcar):
    tt, width = x.shape
    lanes = slice(c * width, (c + 1) * width)
    seg_len = tt // SUBLANES
    halo = (CONV_W - 1) * SUBLANES
    sub = lax.broadcasted_iota(jnp.int32, (SUBLANES, 1), 0)
    for i in range(CONV_W - 1):
        rows = slice(i * SUBLANES, (i + 1) * SUBLANES)
        cur = x[tt - halo + i * SUBLANES:tt - halo + (i + 1) * SUBLANES, :]
        xbuf[rows, lanes] = jnp.where(sub == 0, pltpu.roll(prev_s[rows, lanes], 1, axis=0),
                                      pltpu.roll(cur, 1, axis=0))
    prev_s[:, lanes] = x[tt - halo:, :]
    xbuf[halo:, lanes] = x
    xc = cb_ref[:, lanes] + cw_ref[CONV_W - 1:CONV_W, lanes] * x
    for j in range(CONV_W - 1):
        back = CONV_W - 1 - j
        xc = xc + cw_ref[j:j + 1, lanes] * xbuf[pl.ds(halo - back * SUBLANES, tt), lanes]

    xcb = xc.astype(BF16)
    gate_r = _sigmoid(jnp.dot(xcb, wr_ref[c], preferred_element_type=F32) + br_ref[:, lanes])
    gate_i = _sigmoid(jnp.dot(xcb, wi_ref[c], preferred_element_type=F32) + bi_ref[:, lanes])
    is_start = (lax.broadcasted_iota(jnp.int32, (tt, 1), 0) + jnp.where(first, 0, 1)) == 0
    a, u = _decay_and_input(xc, gate_r, gate_i, lam_ref[:, lanes], is_start)

    hh = jnp.zeros((SUBLANES, width), F32)
    pp = jnp.ones((SUBLANES, width), F32)
    local, decay = [], []
    for g in range(seg_len):
        rows = slice(g * SUBLANES, (g + 1) * SUBLANES)
        hh = a[rows, :] * hh + u[rows, :]
        pp = a[rows, :] * pp
        local.append(hh)
        decay.append(pp)
    cy = hcar[0:1, lanes]
    carries = []
    for s in range(SUBLANES):
        carries.append(cy)
        cy = hh[s:s + 1, :] + pp[s:s + 1, :] * cy
    hcar[0:1, lanes] = cy
    cin = jnp.concatenate(carries, axis=0)
    hr = jnp.concatenate([local[g] + decay[g] * cin for g in range(seg_len)], axis=0)
    gated = (hr * jax.nn.gelu(gr)).astype(BF16)
    return jnp.dot(unperm_ref[...], gated, preferred_element_type=F32).astype(BF16), cy


def _rglru_step_kernel(xr_ref, gr_ref, st_ref, h0_ref, cw_ref, cb_ref, wr_ref, wi_ref, br_ref,
                       bi_ref, lam_ref, o_ref, h_ref, *, at_start):
    xr = xr_ref[...]
    xc = cb_ref[...] + cw_ref[CONV_W - 1:CONV_W, :] * xr
    for j in range(CONV_W - 1):
        xc = xc + cw_ref[j:j + 1, :] * st_ref[j]
    gate_r, gate_i = _block_gates(xc, wr_ref, wi_ref, br_ref[...], bi_ref[...])
    a, u = _decay_and_input(xc, gate_r, gate_i, lam_ref[...], True if at_start else None)
    h = a * h0_ref[...] + u
    h_ref[...] = h
    o_ref[...] = (h * jax.nn.gelu(gr_ref[...])).astype(BF16)


def _rglru_step(xr, gr, state_t, h0, rg, at_start):
    n = xr.shape[0]
    full = lambda shape: pl.BlockSpec(shape, lambda i: (0,) * len(shape))
    nchunk = D_MODEL // RG_CHUNK
    return pl.pallas_call(
        functools.partial(_rglru_step_kernel, at_start=at_start),
        grid=(1,),
        in_specs=[full((n, D_MODEL)), full((n, D_MODEL)), full((CONV_W - 1, n, D_MODEL)),
                  full((n, D_MODEL)), full((CONV_W, D_MODEL)), full((1, D_MODEL)),
                  full((nchunk, RG_CHUNK, RG_CHUNK)), full((nchunk, RG_CHUNK, RG_CHUNK)),
                  full((1, D_MODEL)), full((1, D_MODEL)), full((1, D_MODEL))],
        out_specs=[full((n, D_MODEL)), full((n, D_MODEL))],
        out_shape=[jax.ShapeDtypeStruct((n, D_MODEL), BF16),
                   jax.ShapeDtypeStruct((n, D_MODEL), F32)],
        compiler_params=_params("arbitrary"),
        name="rglru_sample",
    )(xr, gr, state_t, h0, *rg)


def _prompt_update(q_ref, k_ref, vt_ref, m_s, l_s, acc_s, diagonal, tq, tk):
    for h in range(ATT_HEADS):
        lanes = slice(h * LANES, (h + 1) * LANES)
        st = lax.dot_general(k_ref[:, lanes], q_ref[:, lanes], (((1,), (1,)), ((), ())),
                             preferred_element_type=F32)
        if diagonal:
            kpos = lax.broadcasted_iota(jnp.int32, (tk, tq), 0)
            qpos = lax.broadcasted_iota(jnp.int32, (tk, tq), 1)
            st = jnp.where(kpos <= qpos, st, -jnp.inf)
        m_prev = m_s[h]
        m_new = jnp.maximum(m_prev, jnp.max(st, axis=0, keepdims=True))
        alpha = jnp.exp(m_prev - m_new)
        p = jnp.exp(st - m_new)
        l_s[h] = alpha * l_s[h] + jnp.sum(p, axis=0, keepdims=True)
        pv = jnp.dot(vt_ref[0, h * HEAD_DIM:(h + 1) * HEAD_DIM, :], p.astype(BF16),
                     preferred_element_type=F32)
        acc_s[h] = alpha * acc_s[h] + pv
        m_s[h] = m_new


def _paged_update(valid, k_refs, v_refs, lf_refs, qb_s, m_s, l_s, acc_s, car_s):
    g = len(k_refs)
    lane = lax.broadcasted_iota(jnp.int32, (N_HEADS, PAGE_SIZE), 1)

    carry_in = car_s[...]
    carry = carry_in
    bias = []
    for j in range(g):
        pref = lf_refs[j][0, 0]
        d = 1
        while d < PAGE_SIZE:
            pref = pref + jnp.where(lane >= d, pltpu.roll(pref, d, axis=1), 0.0)
            d *= 2
        bias.append(carry + pref)
        carry = carry + pref[:, PAGE_SIZE - 1:PAGE_SIZE]
    car_s[...] = jnp.where(valid, carry, carry_in)

    for hg in range(N_HEADS // SUBLANES):
        heads = range(hg * SUBLANES, (hg + 1) * SUBLANES)
        grp = slice(hg * SUBLANES, (hg + 1) * SUBLANES)
        s = jnp.concatenate(
            [jnp.concatenate([jnp.sum(k_refs[j][0, 0, h] * qb_s[h], axis=0, keepdims=True)
                              for h in heads], axis=0) - bias[j][grp, :]
             for j in range(g)], axis=1)
        m_prev = m_s[grp, :]
        m_grp = jnp.where(valid, jnp.maximum(m_prev, jnp.max(s, axis=1, keepdims=True)), m_prev)
        alpha = jnp.where(valid, jnp.exp(m_prev - m_grp), 1.0)
        pr = jnp.where(valid, jnp.exp(s - m_grp), 0.0)
        l_s[grp, :] = alpha * l_s[grp, :] + jnp.sum(pr, axis=1, keepdims=True)
        m_s[grp, :] = m_grp
        for i, h in enumerate(heads):
            a_h = acc_s[h] * alpha[i:i + 1, :]
            for j in range(g):
                a_h = a_h + pr[i:i + 1, j * PAGE_SIZE:(j + 1) * PAGE_SIZE] * v_refs[j][0, 0, h]
            acc_s[h] = a_h


def _rows_to_columns(x):
    n = x.shape[1]
    eye = (lax.broadcasted_iota(jnp.int32, (n, n), 0)
           == lax.broadcasted_iota(jnp.int32, (n, n), 1)).astype(F32).astype(BF16)
    return sum(lax.dot_general(eye, piece.astype(BF16), (((1,), (1,)), ((), ())),
                               preferred_element_type=F32) for piece in _split3(x))


def _paged_finish(qr_ref, kc_ref, vc_ref, lc_ref, o_ref, m_s, l_s, acc_s, car_s):
    m_past = m_s[...]
    s_c = jnp.sum(qr_ref[0] * kc_ref[0], axis=1, keepdims=True) - (car_s[...] + lc_ref[0])
    m_f = jnp.maximum(m_past, s_c)
    al = jnp.exp(m_past - m_f)
    p_c = jnp.exp(s_c - m_f)
    inv_l = 1.0 / (al * l_s[...] + p_c)
    lane = lax.broadcasted_iota(jnp.int32, (HEAD_DIM, LANES), 1)
    cols = jnp.zeros((HEAD_DIM, LANES), F32)
    for h in range(N_HEADS):
        cols = jnp.where(lane == h, jnp.sum(acc_s[h], axis=1, keepdims=True), cols)
    pick = (lax.broadcasted_iota(jnp.int32, (N_HEADS, LANES), 0)
            == lax.broadcasted_iota(jnp.int32, (N_HEADS, LANES), 1)).astype(F32).astype(BF16)
    past = sum(lax.dot_general(pick, piece.astype(BF16), (((1,), (1,)), ((), ())),
                               preferred_element_type=F32) for piece in _split3(cols))
    o_ref[0] = (past * al + p_c * vc_ref[0]) * inv_l


def _fox_kernel(qt_ref, kt_ref, pt_ref, q_ref, k_ref, vt_ref, qr_ref, kc_ref, vc_ref, lc_ref,
                *rest, tq, tk, g, n_paged_steps, steps_per_seq):
    k_refs = rest[0:g]
    v_refs = rest[g:2 * g]
    lf_refs = rest[2 * g:3 * g]
    o_ref, os_ref = rest[3 * g], rest[3 * g + 1]
    m_s, l_s, acc_s, qb_s, pm_s, pl_s, pacc_s, car_s = rest[3 * g + 2:]
    t = pl.program_id(2)
    qi = qt_ref[t]
    ki = kt_ref[t]
    step = (pl.program_id(0) * pl.num_programs(1) + pl.program_id(1)) * pl.num_programs(2) + t
    valid = step < n_paged_steps
    page_step = jnp.minimum(step, n_paged_steps - 1) % steps_per_seq

    @pl.when(ki == 0)
    def _():
        m_s[...] = jnp.full_like(m_s, -jnp.inf)
        l_s[...] = jnp.zeros_like(l_s)
        acc_s[...] = jnp.zeros_like(acc_s)

    @pl.when(jnp.logical_and(valid, page_step == 0))
    def _():
        q_cols = _rows_to_columns(qr_ref[0])
        for h in range(N_HEADS):
            qb_s[h] = jnp.broadcast_to(q_cols[:, h:h + 1], qb_s.shape[1:])
        pm_s[...] = jnp.full_like(pm_s, -jnp.inf)
        pl_s[...] = jnp.zeros_like(pl_s)
        pacc_s[...] = jnp.zeros_like(pacc_s)
        car_s[...] = jnp.zeros_like(car_s)

    def both(diagonal):
        _prompt_update(q_ref, k_ref, vt_ref, m_s, l_s, acc_s, diagonal, tq, tk)
        _paged_update(valid, k_refs, v_refs, lf_refs, qb_s, pm_s, pl_s, pacc_s, car_s)

    @pl.when(ki < qi)
    def _():
        both(False)

    @pl.when(ki == qi)
    def _():
        both(True)
        ot = jnp.concatenate([acc_s[h] / l_s[h] for h in range(ATT_HEADS)], axis=0)
        o_ref[...] = ot.T.astype(BF16)

    @pl.when(jnp.logical_and(valid, page_step == steps_per_seq - 1))
    def _():
        _paged_finish(qr_ref, kc_ref, vc_ref, lc_ref, os_ref, pm_s, pl_s, pacc_s, car_s)


def _fox_attention(q_aug, k_aug, vt, n_batch, seq, q, k_cur, v_cur, lf_cur, cache_kt, cache_vt,
                   cache_lft, page_table):
    tq = tk = ATT_TILE
    nq = seq // tq
    pairs = [(i, j) for i in range(nq) for j in range(i + 1)]
    qt = jnp.asarray(np.array([p[0] for p in pairs], np.int32))
    kt = jnp.asarray(np.array([p[1] for p in pairs], np.int32))
    groups = N_HEADS // ATT_HEADS
    wide = ATT_HEADS * LANES
    narrow = ATT_HEADS * HEAD_DIM

    n, n_pages = page_table.shape
    g = PAGES_PER_STEP
    steps_per_seq = n_pages // g
    n_paged_steps = n * steps_per_seq
    assert n_paged_steps <= n_batch * groups * len(pairs), "not enough grid steps for the cache pages"

    def seq_and_step(b, gi, t):
        s = jnp.minimum((b * groups + gi) * len(pairs) + t, n_paged_steps - 1)
        return s // steps_per_seq, s % steps_per_seq

    def per_seq(shape):
        zeros = (0,) * len(shape)
        return pl.BlockSpec((1,) + shape,
                            lambda b, gi, t, qt, kt, pt: (seq_and_step(b, gi, t)[0],) + zeros)

    def page(tail, j):
        zeros = (0,) * len(tail)

        def index(b, gi, t, qt, kt, pt):
            sq, st = seq_and_step(b, gi, t)
            return (0, pt[sq, st * g + j]) + zeros
        return pl.BlockSpec((1, 1) + tail, index)

    row = per_seq((N_HEADS, HEAD_DIM))
    grid_spec = pltpu.PrefetchScalarGridSpec(
        num_scalar_prefetch=3,
        grid=(n_batch, groups, len(pairs)),
        in_specs=([pl.BlockSpec((tq, wide), lambda b, gi, t, qt, kt, pt: (b * nq + qt[t], gi)),
                   pl.BlockSpec((tk, wide), lambda b, gi, t, qt, kt, pt: (b * nq + kt[t], gi)),
                   pl.BlockSpec((1, narrow, tk), lambda b, gi, t, qt, kt, pt: (b, gi, kt[t])),
                   row, row, row, per_seq((N_HEADS, 1))]
                  + [page((N_HEADS, HEAD_DIM, PAGE_SIZE), j) for j in range(g)]
                  + [page((N_HEADS, HEAD_DIM, PAGE_SIZE), j) for j in range(g)]
                  + [page((N_HEADS, PAGE_SIZE), j) for j in range(g)]),
        out_specs=[pl.BlockSpec((tq, narrow), lambda b, gi, t, qt, kt, pt: (b * nq + qt[t], gi)),
                   row],
        scratch_shapes=[pltpu.VMEM((ATT_HEADS, 1, tq), F32), pltpu.VMEM((ATT_HEADS, 1, tq), F32),
                        pltpu.VMEM((ATT_HEADS, HEAD_DIM, tq), F32),
                        pltpu.VMEM((N_HEADS, HEAD_DIM, PAGE_SIZE), F32),
                        pltpu.VMEM((N_HEADS, 1), F32), pltpu.VMEM((N_HEADS, 1), F32),
                        pltpu.VMEM((N_HEADS, HEAD_DIM, PAGE_SIZE), F32),
                        pltpu.VMEM((N_HEADS, 1), F32)],
    )
    return pl.pallas_call(
        functools.partial(_fox_kernel, tq=tq, tk=tk, g=g, n_paged_steps=n_paged_steps,
                          steps_per_seq=steps_per_seq),
        grid_spec=grid_spec,
        out_shape=[jax.ShapeDtypeStruct((n_batch * seq, D_MODEL), BF16),
                   jax.ShapeDtypeStruct((n, N_HEADS, HEAD_DIM), F32)],
        compiler_params=_params("arbitrary", "arbitrary", "arbitrary"),
        name="fox_attention",
    )(qt, kt, page_table, q_aug, k_aug, vt, q, k_cur, v_cur, lf_cur,
      *([cache_kt] * g), *([cache_vt] * g), *([cache_lft] * g))


def _merge_kernel(x_ref, a_ref, b_ref, ga_ref, gb_ref, g1_ref, sh_ref, sc_ref, n2_ref,
                  wa_ref, wb_ref, wo_ref, x1_ref, h2_ref):
    ya = jnp.dot(a_ref[...], wa_ref[...], preferred_element_type=F32)
    yb = jnp.dot(b_ref[...], wb_ref[...], preferred_element_type=F32)
    mix = (_sigmoid(ga_ref[...]) * ya + _sigmoid(gb_ref[...]) * yb).astype(BF16)
    x1 = x_ref[...] + g1_ref[0] * jnp.dot(mix, wo_ref[...], preferred_element_type=F32)
    x1_ref[...] = x1
    h2_ref[...] = (_rms(x1, n2_ref[...]) * (1.0 + sc_ref[0]) + sh_ref[0]).astype(BF16)


def _merge(x, a_in, b_in, ga, gb, mod, rows_per_group, tm, norm_g, wa, wb, wo):
    m = x.shape[0]
    r = mod.shape[1]
    blk = pl.BlockSpec((tm, D_MODEL), lambda i: (i, 0))
    sq = _resident((D_MODEL, D_MODEL))
    return pl.pallas_call(
        _merge_kernel,
        grid=(m // tm,),
        in_specs=[blk, blk, blk, blk, blk,
                  _mod_spec(rows_per_group, tm, r, 2), _mod_spec(rows_per_group, tm, r, 3),
                  _mod_spec(rows_per_group, tm, r, 4), _resident((1, D_MODEL)), sq, sq, sq],
        out_specs=[blk, blk],
        out_shape=[jax.ShapeDtypeStruct((m, D_MODEL), F32), jax.ShapeDtypeStruct((m, D_MODEL), BF16)],
        compiler_params=_params("arbitrary"),
        name="merge_proj",
    )(x, a_in, b_in, ga, gb, mod, mod, mod, norm_g, wa, wb, wo)


def _ffn_kernel(x1_ref, h2_ref, g2_ref, nf_ref, wi_ref, wo_ref, y_ref):
    h2 = h2_ref[...]
    gu = jnp.dot(h2, wi_ref[:, 0:D_FF], preferred_element_type=F32)
    up = jnp.dot(h2, wi_ref[:, D_FF:2 * D_FF], preferred_element_type=F32)
    act = (gu * _sigmoid(gu) * up).astype(BF16)
    x2 = x1_ref[...] + g2_ref[0] * jnp.dot(act, wo_ref[...], preferred_element_type=F32)
    y_ref[...] = _rms(x2, nf_ref[...])


def _ffn(x1, h2, mod, rows_per_group, tm, normf_g, w_in, w_out):
    m = x1.shape[0]
    r = mod.shape[1]
    blk = pl.BlockSpec((tm, D_MODEL), lambda i: (i, 0))
    return pl.pallas_call(
        _ffn_kernel,
        grid=(m // tm,),
        in_specs=[blk, blk, _mod_spec(rows_per_group, tm, r, 5), _resident((1, D_MODEL)),
                  _resident((D_MODEL, 2 * D_FF)), _resident((D_FF, D_MODEL))],
        out_specs=blk,
        out_shape=jax.ShapeDtypeStruct((m, D_MODEL), F32),
        compiler_params=_params("arbitrary"),
        name="ffn_final",
    )(x1, h2, mod, normf_g, w_in, w_out)


def _pair_blocks(w):
    per = RG_CHUNK // (D_MODEL // N_BLK)
    bw = D_MODEL // N_BLK
    w = w.reshape(N_BLK // per, per, bw, bw)
    eye = jnp.eye(per, dtype=w.dtype)
    return jnp.einsum("cpij,pq->cpiqj", w, eye).reshape(N_BLK // per, RG_CHUNK, RG_CHUNK).astype(BF16)


def kernel(x_prompt, x_sample, c_prompt, c_sample, cache_k, cache_v, cache_logf, state_conv,
           state_rglru, page_table, ada_w, ada_b, norm1_g, norm2_g, normf_g, w_in, b_f, conv_w,
           conv_b, rg_wr, rg_br, rg_wi, rg_bi, rg_lambda, w_proj_a, w_proj_b, w_o, w_ffn_in,
           w_ffn_out):
    depth = ada_w.shape[0]
    assert depth == 1, "single-layer trunk"
    bsz, seq, d = x_prompt.shape
    nsmp = x_sample.shape[0]
    assert x_sample.shape[1] == 1 and d == D_MODEL
    past_len = page_table.shape[1] * PAGE_SIZE
    attn_w = N_HEADS * HEAD_DIM
    hd = (N_HEADS, HEAD_DIM)

    wl = w_in[0]
    cut = 2 * D_MODEL + 3 * attn_w
    w_all = jnp.concatenate(
        [wl[:, :cut], wl[:, cut + N_HEADS:], jnp.pad(wl[:, cut:cut + N_HEADS], ((0, 0), (0, LANES - N_HEADS)))],
        axis=1).astype(BF16)
    bf_pad = jnp.pad(b_f[0], (0, LANES - N_HEADS)).reshape(1, LANES)
    rg = (conv_w[0], conv_b[0].reshape(1, d), _pair_blocks(rg_wr[0]), _pair_blocks(rg_wi[0]),
          rg_br[0].reshape(1, d), rg_bi[0].reshape(1, d), rg_lambda[0].reshape(1, d))
    wa, wb, wo = (w[0].astype(BF16) for w in (w_proj_a, w_proj_b, w_o))
    wfi, wfo = w_ffn_in[0].astype(BF16), w_ffn_out[0].astype(BF16)
    n1, n2, nf = norm1_g[0].reshape(1, d), norm2_g[0].reshape(1, d), normf_g.reshape(1, d)

    mod = _ada(jnp.concatenate([c_prompt, c_sample], axis=0), ada_w[0], ada_b[0])
    mod_p = mod[:bsz].reshape(bsz, 1, 6 * d)
    mod_s = mod[bsz:].reshape(1, nsmp, 6 * d)

    xp = x_prompt.reshape(bsz * seq, d)
    ga, gb, lf, q_aug, k_aug, kt, vt, vtb, a_in, h_last_p, conv_p = _in_proj(
        xp, mod_p, seq, ROW_TILE, n1, w_all, bf_pad, rg)
    logf_p = lf[:, :N_HEADS].reshape(bsz, seq, N_HEADS)
    xs = x_sample.reshape(nsmp, d)
    xr_s, gr_s, ga_s, gb_s, lf_s, q_s, k_s, v_s = _in_proj(xs, mod_s, nsmp, nsmp, n1, w_all, bf_pad)
    logf_s = lf_s[:, :N_HEADS]
    a_in_s, h_s = _rglru_step(xr_s, gr_s, jnp.transpose(state_conv[0], (1, 0, 2)), state_rglru[0],
                              rg, past_len == 0)

    att, att_s = _fox_attention(
        q_aug, k_aug, vtb, bsz, seq,
        q_s.astype(F32).reshape(nsmp, *hd), k_s.reshape(nsmp, *hd), v_s.reshape(nsmp, *hd),
        logf_s.reshape(nsmp, N_HEADS, 1),
        jnp.transpose(cache_k, (0, 1, 3, 4, 2)), jnp.transpose(cache_v, (0, 1, 3, 4, 2)),
        jnp.transpose(cache_logf, (0, 1, 3, 2)), page_table)

    k_p = jnp.transpose(kt.reshape(1, bsz, *hd, seq), (0, 1, 4, 2, 3))
    v_p = jnp.transpose(vt.reshape(1, bsz, *hd, seq), (0, 1, 4, 2, 3))
    x1, h2 = _merge(xp, a_in, att, ga, gb, mod_p, seq, DENSE_TILE, n2, wa, wb, wo)
    y_p = _ffn(x1, h2, mod_p, seq, DENSE_TILE, nf, wfi, wfo).reshape(bsz, seq, d)

    x1_s, h2_s = _merge(xs, a_in_s, att_s.reshape(nsmp, d).astype(BF16), ga_s, gb_s, mod_s, nsmp,
                        nsmp, n2, wa, wb, wo)
    y_s = _ffn(x1_s, h2_s, mod_s, nsmp, nsmp, nf, wfi, wfo).reshape(nsmp, 1, d)
    conv_s = jnp.concatenate([state_conv[0][:, 1:], xr_s[:, None, :]], axis=1)

    return (y_p, y_s,
            k_p, v_p, logf_p[None],
            conv_p[None], h_last_p.reshape(1, bsz, d),
            k_s.reshape(1, nsmp, 1, *hd), v_s.reshape(1, nsmp, 1, *hd),
            logf_s.reshape(1, nsmp, 1, N_HEADS),
            conv_s[None], h_s[None])
```

```python
import functools

import numpy as np
import jax
import jax.numpy as jnp
from jax import lax
from jax.experimental import pallas as pl
from jax.experimental.pallas import tpu as pltpu

F32 = jnp.float32
BF16 = jnp.bfloat16

D_MODEL = 1024
N_HEADS = 16
HEAD_DIM = 64
N_BLK = 16
CONV_W = 4
RG_C = 8.0
PAGE_SIZE = 128
D_FF = 2816
EPS = 1e-6

LANES = 128
SUBLANES = 8
MXU_DIM = 256
VMEM_LIMIT = 56 * 1024 * 1024

ROW_TILE = 256
DENSE_TILE = 512
ATT_TILE = 512
ATT_HEADS = 8
PAGES_PER_STEP = 8

RG_CHUNK = 128
N_SEG = 7
W_ALL_COLS = N_SEG * D_MODEL + LANES

AUG = HEAD_DIM
N_SPLIT = 3
ONES_LANE = N_SPLIT * N_HEADS


def _params(*sem):
    return pltpu.CompilerParams(dimension_semantics=sem, vmem_limit_bytes=VMEM_LIMIT)


def _resident(shape):
    nd = len(shape)
    return pl.BlockSpec(shape, lambda *_: (0,) * nd, pipeline_mode=pl.Buffered(1))


def _sigmoid(x):
    return 1.0 / (1.0 + jnp.exp(-x))


def _log_sigmoid(x):
    return jnp.minimum(x, 0.0) - jnp.log1p(jnp.exp(-jnp.abs(x)))


def _softplus(x):
    return jnp.maximum(x, 0.0) + jnp.log1p(jnp.exp(-jnp.abs(x)))


def _rms(x, g):
    return x * lax.rsqrt(jnp.mean(x * x, axis=-1, keepdims=True) + EPS) * g


def _ada_kernel(c_ref, w_ref, b_ref, o_ref):
    c = c_ref[...]
    s = (c * _sigmoid(c)).astype(BF16)
    o_ref[...] = jnp.dot(s, w_ref[...].astype(BF16), preferred_element_type=F32) + b_ref[...]


def _ada(c, w, b):
    n, d = c.shape
    cols = w.shape[1]
    return pl.pallas_call(
        _ada_kernel,
        grid=(cols // d,),
        in_specs=[pl.BlockSpec((n, d), lambda j: (0, 0)),
                  pl.BlockSpec((d, d), lambda j: (0, j)),
                  pl.BlockSpec((1, d), lambda j: (0, j))],
        out_specs=pl.BlockSpec((n, d), lambda j: (0, j)),
        out_shape=jax.ShapeDtypeStruct((n, cols), F32),
        compiler_params=_params("arbitrary"),
        name="ada_modulation",
    )(c, w, b.reshape(1, cols))


def _mod_spec(rows_per_group, tm, r, chunk):
    tiles = rows_per_group // tm
    return pl.BlockSpec((1, r, D_MODEL), lambda i: (i // tiles, 0, chunk))


def _split3(x):
    hi = x.astype(BF16).astype(F32)
    r = x - hi
    mid = r.astype(BF16).astype(F32)
    return hi, mid, r - mid


def _in_proj_kernel(*refs, prompt, tiles_per_seq):
    if prompt:
        (x_ref, g_ref, sh_ref, sc_ref, w_ref, bf_ref, tri_ref, pq_ref, pk_ref, perm_ref, unperm_ref,
         cw_ref, cb_ref, wr_ref, wi_ref, br_ref, bi_ref, lam_ref,
         ga_ref, gb_ref, lf_ref, qa_ref, ka_ref, kt_ref, vt_ref, vtb_ref, ain_ref, hl_ref, ct_ref,
         fcar, xbuf, prev_s, hcar) = refs
    else:
        (x_ref, g_ref, sh_ref, sc_ref, w_ref, bf_ref,
         xr_ref, gr_ref, ga_ref, gb_ref, lf_ref, q_ref, k_ref, v_ref) = refs
    x = x_ref[...]
    h = (_rms(x, g_ref[...]) * (1.0 + sc_ref[0]) + sh_ref[0]).astype(BF16)

    def seg(s, lhs=h):
        return jnp.dot(lhs, w_ref[:, s * D_MODEL:(s + 1) * D_MODEL], preferred_element_type=F32)

    if prompt:
        first = pl.program_id(0) % tiles_per_seq == 0

        @pl.when(first)
        def _():
            fcar[...] = jnp.zeros_like(fcar)
            prev_s[...] = jnp.zeros_like(prev_s)
            hcar[...] = jnp.zeros_like(hcar)

        hs = jnp.dot(perm_ref[...], h, preferred_element_type=F32).astype(BF16)
        xr = seg(0, hs)
        gr = seg(1, hs)
        n_rows = xr.shape[0]
        ct_ref[0] = jnp.concatenate(
            [xr[n_rows - 1 - i * SUBLANES:n_rows - i * SUBLANES, :]
             for i in reversed(range(CONV_W - 1))], axis=0)

        def rglru_chunk(c):
            lanes = slice(c * RG_CHUNK, (c + 1) * RG_CHUNK)
            ain_ref[:, lanes], hl_ref[0, :, lanes] = _rglru_tile(
                xr[:, lanes], gr[:, lanes], first, c, unperm_ref, cw_ref, cb_ref, wr_ref, wi_ref,
                br_ref, bi_ref, lam_ref, xbuf, prev_s, hcar)
        n_chunks = D_MODEL // RG_CHUNK
    else:
        xr_ref[...] = seg(0)
        gr_ref[...] = seg(1)
        n_chunks = 0

    half = D_MODEL // 2
    halves = {}
    for n, (s, i) in enumerate((s, i) for s in range(2, N_SEG) for i in range(2)):
        if n < n_chunks:
            rglru_chunk(n)
        cols = slice(s * D_MODEL + i * half, s * D_MODEL + (i + 1) * half)
        halves[s, i] = jnp.dot(h, w_ref[:, cols], preferred_element_type=F32)
    q, k, v, ga, gb = (jnp.concatenate([halves[s, 0], halves[s, 1]], axis=1)
                       for s in range(2, N_SEG))
    q = (q * (HEAD_DIM ** -0.5)).astype(BF16)
    ga_ref[...] = ga
    gb_ref[...] = gb
    fl = jnp.dot(h, w_ref[:, N_SEG * D_MODEL:], preferred_element_type=F32)
    lf = _log_sigmoid(fl + bf_ref[...])
    lf_ref[...] = lf
    if not prompt:
        q_ref[...] = q
        k_ref[...] = k
        v_ref[...] = v
        return

    tm = x.shape[0]
    kt_ref[0] = k.T
    vt = v.T
    vt_ref[0] = vt
    vtb_ref[0] = vt.astype(BF16)

    lane = lax.broadcasted_iota(jnp.int32, (1, LANES), 1)
    tri = tri_ref[...]
    cum = fcar[...]
    for piece in _split3(jnp.where(lane < N_HEADS, lf, 0.0)):
        cum = cum + jnp.dot(tri, piece.astype(BF16), preferred_element_type=F32)
    fcar[...] = cum[tm - 1:tm, :]

    c_hi, c_mid, c_lo = _split3(cum)
    faug = jnp.where(lane < N_HEADS, c_hi,
                     jnp.where(lane < 2 * N_HEADS, pltpu.roll(c_mid, N_HEADS, axis=1),
                               jnp.where(lane < ONES_LANE, pltpu.roll(c_lo, 2 * N_HEADS, axis=1),
                                         jnp.where(lane == ONES_LANE, 1.0, 0.0)))).astype(BF16)
    kb = k.astype(BF16)
    for j in range(N_HEADS // 2):
        pair = slice(j * LANES, (j + 1) * LANES)
        out = slice(j * MXU_DIM, (j + 1) * MXU_DIM)
        qa_ref[:, out] = jnp.dot(jnp.concatenate([q[:, pair], faug], axis=1), pq_ref[j],
                                 preferred_element_type=F32).astype(BF16)
        ka_ref[:, out] = jnp.dot(jnp.concatenate([kb[:, pair], faug], axis=1), pk_ref[j],
                                 preferred_element_type=F32).astype(BF16)


def _scan_order(tile):
    seg_len = tile // SUBLANES
    n = np.arange(tile)
    perm = np.zeros((tile, tile), np.float32)
    perm[n, (n % SUBLANES) * seg_len + n // SUBLANES] = 1.0
    return perm


def _placement_matrices():
    n_pairs = N_HEADS // 2
    pq = np.zeros((n_pairs, MXU_DIM, MXU_DIM), np.float32)
    pk = np.zeros((n_pairs, MXU_DIM, MXU_DIM), np.float32)
    for j in range(n_pairs):
        for s in range(2):
            head = 2 * j + s
            for d in range(HEAD_DIM):
                pq[j, s * HEAD_DIM + d, s * LANES + d] = 1.0
                pk[j, s * HEAD_DIM + d, s * LANES + d] = 1.0
            cb = s * LANES + AUG
            for i in range(N_SPLIT):
                pq[j, LANES + i * N_HEADS + head, cb + i] = 1.0
                pq[j, LANES + ONES_LANE, cb + N_SPLIT + i] = 1.0
                pk[j, LANES + ONES_LANE, cb + i] = 1.0
                pk[j, LANES + i * N_HEADS + head, cb + N_SPLIT + i] = -1.0
    return jnp.asarray(pq, BF16), jnp.asarray(pk, BF16)


def _in_proj(x, mod, rows_per_group, tm, norm_g, w_all, bf_pad, rg=None):
    prompt = rg is not None
    m = x.shape[0]
    r = mod.shape[1]
    row = lambda i: (i, 0)
    f32o = jax.ShapeDtypeStruct((m, D_MODEL), F32)
    bf16o = jax.ShapeDtypeStruct((m, D_MODEL), BF16)
    blk = pl.BlockSpec((tm, D_MODEL), row)
    in_specs = [blk, _resident((1, D_MODEL)),
                _mod_spec(rows_per_group, tm, r, 0), _mod_spec(rows_per_group, tm, r, 1),
                _resident((D_MODEL, W_ALL_COLS)), _resident((1, LANES))]
    operands = [x, norm_g, mod, mod, w_all, bf_pad]
    lf_spec = pl.BlockSpec((tm, LANES), row)
    lf_shape = jax.ShapeDtypeStruct((m, LANES), F32)
    scratch = []
    if prompt:
        tiles = rows_per_group // tm
        n_seq = m // rows_per_group
        halo = (CONV_W - 1) * SUBLANES
        tri = jnp.asarray(np.tril(np.ones((tm, tm), np.float32)), BF16)
        pq, pk = _placement_matrices()
        perm = _scan_order(tm)
        consts = [tri, pq, pk, jnp.asarray(perm, BF16), jnp.asarray(perm.T, BF16)]
        in_specs += [_resident(c.shape) for c in consts] + [_resident(p.shape) for p in rg]
        operands += consts + list(rg)
        wide = pl.BlockSpec((tm, 2 * D_MODEL), row)
        time_on_lanes = pl.BlockSpec((1, D_MODEL, tm), lambda i: (i // tiles, 0, i % tiles))
        per_seq = lambda rows: pl.BlockSpec((1, rows, D_MODEL), lambda i: (i // tiles, 0, 0))
        out_specs = ([blk, blk, lf_spec, wide, wide] + [time_on_lanes] * 3
                     + [blk, per_seq(1), per_seq(CONV_W - 1)])
        out_shape = ([f32o, f32o, lf_shape] + [jax.ShapeDtypeStruct((m, 2 * D_MODEL), BF16)] * 2
                     + [jax.ShapeDtypeStruct((n_seq, D_MODEL, rows_per_group), F32)] * 2
                     + [jax.ShapeDtypeStruct((n_seq, D_MODEL, rows_per_group), BF16), bf16o,
                        jax.ShapeDtypeStruct((n_seq, 1, D_MODEL), F32),
                        jax.ShapeDtypeStruct((n_seq, CONV_W - 1, D_MODEL), F32)])
        scratch = [pltpu.VMEM((1, LANES), F32), pltpu.VMEM((tm + halo, D_MODEL), F32),
                   pltpu.VMEM((halo, D_MODEL), F32), pltpu.VMEM((SUBLANES, D_MODEL), F32)]
    else:
        out_specs = [blk] * 4 + [lf_spec] + [blk] * 3
        out_shape = [f32o] * 4 + [lf_shape, bf16o, f32o, f32o]
    return pl.pallas_call(
        functools.partial(_in_proj_kernel, prompt=prompt, tiles_per_seq=rows_per_group // tm),
        grid=(m // tm,),
        in_specs=in_specs, out_specs=out_specs, out_shape=out_shape, scratch_shapes=scratch,
        compiler_params=_params("arbitrary"),
        name="in_proj_prompt" if prompt else "in_proj_sample",
    )(*operands)


def _block_gates(xc, wr_ref, wi_ref, br, bi):
    xcb = xc.astype(BF16)
    n = D_MODEL // RG_CHUNK
    pr = [jnp.dot(xcb[:, c * RG_CHUNK:(c + 1) * RG_CHUNK], wr_ref[c], preferred_element_type=F32)
          for c in range(n)]
    pi = [jnp.dot(xcb[:, c * RG_CHUNK:(c + 1) * RG_CHUNK], wi_ref[c], preferred_element_type=F32)
          for c in range(n)]
    gate_r = _sigmoid(jnp.concatenate(pr, axis=1) + br)
    gate_i = _sigmoid(jnp.concatenate(pi, axis=1) + bi)
    return gate_r, gate_i


def _decay_and_input(xc, gate_r, gate_i, lam, is_start):
    log_a = -RG_C * gate_r * _softplus(-lam)
    a = jnp.exp(log_a)
    mult = jnp.sqrt(jnp.tanh(-log_a) * (a * a + 1.0))
    if is_start is not None:
        mult = jnp.where(is_start, 1.0, mult)
    return a, mult * gate_i * xc


def _rglru_tile(x, gr, first, c, unperm_ref, cw_ref, cb_ref, wr_ref, wi_ref, br_ref, bi_ref,
                lam_ref, xbuf, prev_s, hcar):
    tt, width = x.shape
    lanes = slice(c * width, (c + 1) * width)
    seg_len = tt // SUBLANES
    halo = (CONV_W - 1) * SUBLANES
    sub = lax.broadcasted_iota(jnp.int32, (SUBLANES, 1), 0)
    for i in range(CONV_W - 1):
        rows = slice(i * SUBLANES, (i + 1) * SUBLANES)
        cur = x[tt - halo + i * SUBLANES:tt - halo + (i + 1) * SUBLANES, :]
        xbuf[rows, lanes] = jnp.where(sub == 0, pltpu.roll(prev_s[rows, lanes], 1, axis=0),
                                      pltpu.roll(cur, 1, axis=0))
    prev_s[:, lanes] = x[tt - halo:, :]
    xbuf[halo:, lanes] = x
    xc = cb_ref[:, lanes] + cw_ref[CONV_W - 1:CONV_W, lanes] * x
    for j in range(CONV_W - 1):
        back = CONV_W - 1 - j
        xc = xc + cw_ref[j:j + 1, lanes] * xbuf[pl.ds(halo - back * SUBLANES, tt), lanes]

    xcb = xc.astype(BF16)
    gate_r = _sigmoid(jnp.dot(xcb, wr_ref[c], preferred_element_type=F32) + br_ref[:, lanes])
    gate_i = _sigmoid(jnp.dot(xcb, wi_ref[c], preferred_element_type=F32) + bi_ref[:, lanes])
    is_start = (lax.broadcasted_iota(jnp.int32, (tt, 1), 0) + jnp.where(first, 0, 1)) == 0
    a, u = _decay_and_input(xc, gate_r, gate_i, lam_ref[:, lanes], is_start)

    hh = jnp.zeros((SUBLANES, width), F32)
    pp = jnp.ones((SUBLANES, width), F32)
    local, decay = [], []
    for g in range(seg_len):
        rows = slice(g * SUBLANES, (g + 1) * SUBLANES)
        hh = a[rows, :] * hh + u[rows, :]
        pp = a[rows, :] * pp
        local.append(hh)
        decay.append(pp)
    cy = hcar[0:1, lanes]
    carries = []
    for s in range(SUBLANES):
        carries.append(cy)
        cy = hh[s:s + 1, :] + pp[s:s + 1, :] * cy
    hcar[0:1, lanes] = cy
    cin = jnp.concatenate(carries, axis=0)
    hr = jnp.concatenate([local[g] + decay[g] * cin for g in range(seg_len)], axis=0)
    gated = (hr * jax.nn.gelu(gr)).astype(BF16)
    return jnp.dot(unperm_ref[...], gated, preferred_element_type=F32).astype(BF16), cy


def _rglru_step_kernel(xr_ref, gr_ref, st_ref, h0_ref, cw_ref, cb_ref, wr_ref, wi_ref, br_ref,
                       bi_ref, lam_ref, o_ref, h_ref, *, at_start):
    xr = xr_ref[...]
    xc = cb_ref[...] + cw_ref[CONV_W - 1:CONV_W, :] * xr
    for j in range(CONV_W - 1):
        xc = xc + cw_ref[j:j + 1, :] * st_ref[j]
    gate_r, gate_i = _block_gates(xc, wr_ref, wi_ref, br_ref[...], bi_ref[...])
    a, u = _decay_and_input(xc, gate_r, gate_i, lam_ref[...], True if at_start else None)
    h = a * h0_ref[...] + u
    h_ref[...] = h
    o_ref[...] = (h * jax.nn.gelu(gr_ref[...])).astype(BF16)


def _rglru_step(xr, gr, state_t, h0, rg, at_start):
    n = xr.shape[0]
    full = lambda shape: pl.BlockSpec(shape, lambda i: (0,) * len(shape))
    nchunk = D_MODEL // RG_CHUNK
    return pl.pallas_call(
        functools.partial(_rglru_step_kernel, at_start=at_start),
        grid=(1,),
        in_specs=[full((n, D_MODEL)), full((n, D_MODEL)), full((CONV_W - 1, n, D_MODEL)),
                  full((n, D_MODEL)), full((CONV_W, D_MODEL)), full((1, D_MODEL)),
                  full((nchunk, RG_CHUNK, RG_CHUNK)), full((nchunk, RG_CHUNK, RG_CHUNK)),
                  full((1, D_MODEL)), full((1, D_MODEL)), full((1, D_MODEL))],
        out_specs=[full((n, D_MODEL)), full((n, D_MODEL))],
        out_shape=[jax.ShapeDtypeStruct((n, D_MODEL), BF16),
                   jax.ShapeDtypeStruct((n, D_MODEL), F32)],
        compiler_params=_params("arbitrary"),
        name="rglru_sample",
    )(xr, gr, state_t, h0, *rg)


def _prompt_update(q_ref, k_ref, vt_ref, m_s, l_s, acc_s, diagonal, tq, tk):
    for h in range(ATT_HEADS):
        lanes = slice(h * LANES, (h + 1) * LANES)
        st = lax.dot_general(k_ref[:, lanes], q_ref[:, lanes], (((1,), (1,)), ((), ())),
                             preferred_element_type=F32)
        if diagonal:
            kpos = lax.broadcasted_iota(jnp.int32, (tk, tq), 0)
            qpos = lax.broadcasted_iota(jnp.int32, (tk, tq), 1)
            st = jnp.where(kpos <= qpos, st, -jnp.inf)
        m_prev = m_s[h]
        m_new = jnp.maximum(m_prev, jnp.max(st, axis=0, keepdims=True))
        alpha = jnp.exp(m_prev - m_new)
        p = jnp.exp(st - m_new).astype(BF16)
        v_ones = jnp.concatenate([vt_ref[0, h * HEAD_DIM:(h + 1) * HEAD_DIM, :],
                                  jnp.ones((2 * SUBLANES, tk), BF16)], axis=0)
        pv = jnp.dot(v_ones, p, preferred_element_type=F32)
        l_s[h] = alpha * l_s[h] + pv[HEAD_DIM:HEAD_DIM + 1, :]
        acc_s[h] = alpha * acc_s[h] + pv[0:HEAD_DIM, :]
        m_s[h] = m_new


def _paged_update(valid, k_refs, v_refs, lf_refs, qb_s, m_s, l_s, acc_s, car_s):
    g = len(k_refs)
    lane = lax.broadcasted_iota(jnp.int32, (N_HEADS, PAGE_SIZE), 1)

    carry_in = car_s[...]
    carry = carry_in
    bias = []
    for j in range(g):
        pref = lf_refs[j][0, 0]
        d = 1
        while d < PAGE_SIZE:
            pref = pref + jnp.where(lane >= d, pltpu.roll(pref, d, axis=1), 0.0)
            d *= 2
        bias.append(carry + pref)
        carry = carry + pref[:, PAGE_SIZE - 1:PAGE_SIZE]
    car_s[...] = jnp.where(valid, carry, carry_in)

    for hg in range(N_HEADS // SUBLANES):
        heads = range(hg * SUBLANES, (hg + 1) * SUBLANES)
        grp = slice(hg * SUBLANES, (hg + 1) * SUBLANES)
        s = jnp.concatenate(
            [jnp.concatenate([jnp.sum(k_refs[j][0, 0, h] * qb_s[h], axis=0, keepdims=True)
                              for h in heads], axis=0) - bias[j][grp, :]
             for j in range(g)], axis=1)
        m_prev = m_s[grp, :]
        m_grp = jnp.where(valid, jnp.maximum(m_prev, jnp.max(s, axis=1, keepdims=True)), m_prev)
        alpha = jnp.where(valid, jnp.exp(m_prev - m_grp), 1.0)
        pr = jnp.where(valid, jnp.exp(s - m_grp), 0.0)
        l_s[grp, :] = alpha * l_s[grp, :] + jnp.sum(pr, axis=1, keepdims=True)
        m_s[grp, :] = m_grp
        for i, h in enumerate(heads):
            a_h = acc_s[h] * alpha[i:i + 1, :]
            for j in range(g):
                a_h = a_h + pr[i:i + 1, j * PAGE_SIZE:(j + 1) * PAGE_SIZE] * v_refs[j][0, 0, h]
            acc_s[h] = a_h


def _rows_to_columns(x):
    n = x.shape[1]
    eye = (lax.broadcasted_iota(jnp.int32, (n, n), 0)
           == lax.broadcasted_iota(jnp.int32, (n, n), 1)).astype(F32).astype(BF16)
    return sum(lax.dot_general(eye, piece.astype(BF16), (((1,), (1,)), ((), ())),
                               preferred_element_type=F32) for piece in _split3(x))


def _paged_finish(qr_ref, kc_ref, vc_ref, lc_ref, o_ref, m_s, l_s, acc_s, car_s):
    m_past = m_s[...]
    s_c = jnp.sum(qr_ref[0] * kc_ref[0], axis=1, keepdims=True) - (car_s[...] + lc_ref[0])
    m_f = jnp.maximum(m_past, s_c)
    al = jnp.exp(m_past - m_f)
    p_c = jnp.exp(s_c - m_f)
    inv_l = 1.0 / (al * l_s[...] + p_c)
    lane = lax.broadcasted_iota(jnp.int32, (HEAD_DIM, LANES), 1)
    cols = jnp.zeros((HEAD_DIM, LANES), F32)
    for h in range(N_HEADS):
        cols = jnp.where(lane == h, jnp.sum(acc_s[h], axis=1, keepdims=True), cols)
    pick = (lax.broadcasted_iota(jnp.int32, (N_HEADS, LANES), 0)
            == lax.broadcasted_iota(jnp.int32, (N_HEADS, LANES), 1)).astype(F32).astype(BF16)
    past = sum(lax.dot_general(pick, piece.astype(BF16), (((1,), (1,)), ((), ())),
                               preferred_element_type=F32) for piece in _split3(cols))
    o_ref[0] = (past * al + p_c * vc_ref[0]) * inv_l


def _fox_kernel(qt_ref, kt_ref, pt_ref, q_ref, k_ref, vt_ref, qr_ref, kc_ref, vc_ref, lc_ref,
                *rest, tq, tk, g, n_paged_steps, steps_per_seq):
    k_refs = rest[0:g]
    v_refs = rest[g:2 * g]
    lf_refs = rest[2 * g:3 * g]
    o_ref, os_ref = rest[3 * g], rest[3 * g + 1]
    m_s, l_s, acc_s, qb_s, pm_s, pl_s, pacc_s, car_s = rest[3 * g + 2:]
    t = pl.program_id(2)
    qi = qt_ref[t]
    ki = kt_ref[t]
    step = (pl.program_id(0) * pl.num_programs(1) + pl.program_id(1)) * pl.num_programs(2) + t
    valid = step < n_paged_steps
    page_step = jnp.minimum(step, n_paged_steps - 1) % steps_per_seq

    @pl.when(ki == 0)
    def _():
        m_s[...] = jnp.full_like(m_s, -jnp.inf)
        l_s[...] = jnp.zeros_like(l_s)
        acc_s[...] = jnp.zeros_like(acc_s)

    @pl.when(jnp.logical_and(valid, page_step == 0))
    def _():
        q_cols = _rows_to_columns(qr_ref[0])
        for h in range(N_HEADS):
            qb_s[h] = jnp.broadcast_to(q_cols[:, h:h + 1], qb_s.shape[1:])
        pm_s[...] = jnp.full_like(pm_s, -jnp.inf)
        pl_s[...] = jnp.zeros_like(pl_s)
        pacc_s[...] = jnp.zeros_like(pacc_s)
        car_s[...] = jnp.zeros_like(car_s)

    def both(diagonal):
        _prompt_update(q_ref, k_ref, vt_ref, m_s, l_s, acc_s, diagonal, tq, tk)
        _paged_update(valid, k_refs, v_refs, lf_refs, qb_s, pm_s, pl_s, pacc_s, car_s)

    @pl.when(ki < qi)
    def _():
        both(False)

    @pl.when(ki == qi)
    def _():
        both(True)
        ot = jnp.concatenate([acc_s[h] / l_s[h] for h in range(ATT_HEADS)], axis=0)
        o_ref[...] = ot.T.astype(BF16)

    @pl.when(jnp.logical_and(valid, page_step == steps_per_seq - 1))
    def _():
        _paged_finish(qr_ref, kc_ref, vc_ref, lc_ref, os_ref, pm_s, pl_s, pacc_s, car_s)


def _fox_attention(q_aug, k_aug, vt, n_batch, seq, q, k_cur, v_cur, lf_cur, cache_kt, cache_vt,
                   cache_lft, page_table):
    tq = tk = ATT_TILE
    nq = seq // tq
    pairs = [(i, j) for i in range(nq) for j in range(i + 1)]
    qt = jnp.asarray(np.array([p[0] for p in pairs], np.int32))
    kt = jnp.asarray(np.array([p[1] for p in pairs], np.int32))
    groups = N_HEADS // ATT_HEADS
    wide = ATT_HEADS * LANES
    narrow = ATT_HEADS * HEAD_DIM

    n, n_pages = page_table.shape
    g = PAGES_PER_STEP
    steps_per_seq = n_pages // g
    n_paged_steps = n * steps_per_seq
    assert n_paged_steps <= n_batch * groups * len(pairs), "not enough grid steps for the cache pages"

    def seq_and_step(b, gi, t):
        s = jnp.minimum((b * groups + gi) * len(pairs) + t, n_paged_steps - 1)
        return s // steps_per_seq, s % steps_per_seq

    def per_seq(shape):
        zeros = (0,) * len(shape)
        return pl.BlockSpec((1,) + shape,
                            lambda b, gi, t, qt, kt, pt: (seq_and_step(b, gi, t)[0],) + zeros)

    def page(tail, j):
        zeros = (0,) * len(tail)

        def index(b, gi, t, qt, kt, pt):
            sq, st = seq_and_step(b, gi, t)
            return (0, pt[sq, st * g + j]) + zeros
        return pl.BlockSpec((1, 1) + tail, index)

    row = per_seq((N_HEADS, HEAD_DIM))
    grid_spec = pltpu.PrefetchScalarGridSpec(
        num_scalar_prefetch=3,
        grid=(n_batch, groups, len(pairs)),
        in_specs=([pl.BlockSpec((tq, wide), lambda b, gi, t, qt, kt, pt: (b * nq + qt[t], gi)),
                   pl.BlockSpec((tk, wide), lambda b, gi, t, qt, kt, pt: (b * nq + kt[t], gi)),
                   pl.BlockSpec((1, narrow, tk), lambda b, gi, t, qt, kt, pt: (b, gi, kt[t])),
                   row, row, row, per_seq((N_HEADS, 1))]
                  + [page((N_HEADS, HEAD_DIM, PAGE_SIZE), j) for j in range(g)]
                  + [page((N_HEADS, HEAD_DIM, PAGE_SIZE), j) for j in range(g)]
                  + [page((N_HEADS, PAGE_SIZE), j) for j in range(g)]),
        out_specs=[pl.BlockSpec((tq, narrow), lambda b, gi, t, qt, kt, pt: (b * nq + qt[t], gi)),
                   row],
        scratch_shapes=[pltpu.VMEM((ATT_HEADS, 1, tq), F32), pltpu.VMEM((ATT_HEADS, 1, tq), F32),
                        pltpu.VMEM((ATT_HEADS, HEAD_DIM, tq), F32),
                        pltpu.VMEM((N_HEADS, HEAD_DIM, PAGE_SIZE), F32),
                        pltpu.VMEM((N_HEADS, 1), F32), pltpu.VMEM((N_HEADS, 1), F32),
                        pltpu.VMEM((N_HEADS, HEAD_DIM, PAGE_SIZE), F32),
                        pltpu.VMEM((N_HEADS, 1), F32)],
    )
    return pl.pallas_call(
        functools.partial(_fox_kernel, tq=tq, tk=tk, g=g, n_paged_steps=n_paged_steps,
                          steps_per_seq=steps_per_seq),
        grid_spec=grid_spec,
        out_shape=[jax.ShapeDtypeStruct((n_batch * seq, D_MODEL), BF16),
                   jax.ShapeDtypeStruct((n, N_HEADS, HEAD_DIM), F32)],
        compiler_params=_params("arbitrary", "arbitrary", "arbitrary"),
        name="fox_attention",
    )(qt, kt, page_table, q_aug, k_aug, vt, q, k_cur, v_cur, lf_cur,
      *([cache_kt] * g), *([cache_vt] * g), *([cache_lft] * g))


def _merge_kernel(x_ref, a_ref, b_ref, ga_ref, gb_ref, g1_ref, sh_ref, sc_ref, n2_ref,
                  wa_ref, wb_ref, wo_ref, x1_ref, h2_ref):
    ya = jnp.dot(a_ref[...], wa_ref[...], preferred_element_type=F32)
    yb = jnp.dot(b_ref[...], wb_ref[...], preferred_element_type=F32)
    mix = (_sigmoid(ga_ref[...]) * ya + _sigmoid(gb_ref[...]) * yb).astype(BF16)
    x1 = x_ref[...] + g1_ref[0] * jnp.dot(mix, wo_ref[...], preferred_element_type=F32)
    x1_ref[...] = x1
    h2_ref[...] = (_rms(x1, n2_ref[...]) * (1.0 + sc_ref[0]) + sh_ref[0]).astype(BF16)


def _merge(x, a_in, b_in, ga, gb, mod, rows_per_group, tm, norm_g, wa, wb, wo):
    m = x.shape[0]
    r = mod.shape[1]
    blk = pl.BlockSpec((tm, D_MODEL), lambda i: (i, 0))
    sq = _resident((D_MODEL, D_MODEL))
    return pl.pallas_call(
        _merge_kernel,
        grid=(m // tm,),
        in_specs=[blk, blk, blk, blk, blk,
                  _mod_spec(rows_per_group, tm, r, 2), _mod_spec(rows_per_group, tm, r, 3),
                  _mod_spec(rows_per_group, tm, r, 4), _resident((1, D_MODEL)), sq, sq, sq],
        out_specs=[blk, blk],
        out_shape=[jax.ShapeDtypeStruct((m, D_MODEL), F32), jax.ShapeDtypeStruct((m, D_MODEL), BF16)],
        compiler_params=_params("arbitrary"),
        name="merge_proj",
    )(x, a_in, b_in, ga, gb, mod, mod, mod, norm_g, wa, wb, wo)


def _ffn_kernel(x1_ref, h2_ref, g2_ref, nf_ref, wi_ref, wo_ref, y_ref):
    h2 = h2_ref[...]
    gu = jnp.dot(h2, wi_ref[:, 0:D_FF], preferred_element_type=F32)
    up = jnp.dot(h2, wi_ref[:, D_FF:2 * D_FF], preferred_element_type=F32)
    act = (gu * _sigmoid(gu) * up).astype(BF16)
    x2 = x1_ref[...] + g2_ref[0] * jnp.dot(act, wo_ref[...], preferred_element_type=F32)
    y_ref[...] = _rms(x2, nf_ref[...])


def _ffn(x1, h2, mod, rows_per_group, tm, normf_g, w_in, w_out):
    m = x1.shape[0]
    r = mod.shape[1]
    blk = pl.BlockSpec((tm, D_MODEL), lambda i: (i, 0))
    return pl.pallas_call(
        _ffn_kernel,
        grid=(m // tm,),
        in_specs=[blk, blk, _mod_spec(rows_per_group, tm, r, 5), _resident((1, D_MODEL)),
                  _resident((D_MODEL, 2 * D_FF)), _resident((D_FF, D_MODEL))],
        out_specs=blk,
        out_shape=jax.ShapeDtypeStruct((m, D_MODEL), F32),
        compiler_params=_params("arbitrary"),
        name="ffn_final",
    )(x1, h2, mod, normf_g, w_in, w_out)


def _pair_blocks(w):
    per = RG_CHUNK // (D_MODEL // N_BLK)
    bw = D_MODEL // N_BLK
    w = w.reshape(N_BLK // per, per, bw, bw)
    eye = jnp.eye(per, dtype=w.dtype)
    return jnp.einsum("cpij,pq->cpiqj", w, eye).reshape(N_BLK // per, RG_CHUNK, RG_CHUNK).astype(BF16)


def kernel(x_prompt, x_sample, c_prompt, c_sample, cache_k, cache_v, cache_logf, state_conv,
           state_rglru, page_table, ada_w, ada_b, norm1_g, norm2_g, normf_g, w_in, b_f, conv_w,
           conv_b, rg_wr, rg_br, rg_wi, rg_bi, rg_lambda, w_proj_a, w_proj_b, w_o, w_ffn_in,
           w_ffn_out):
    depth = ada_w.shape[0]
    assert depth == 1, "single-layer trunk"
    bsz, seq, d = x_prompt.shape
    nsmp = x_sample.shape[0]
    assert x_sample.shape[1] == 1 and d == D_MODEL
    past_len = page_table.shape[1] * PAGE_SIZE
    attn_w = N_HEADS * HEAD_DIM
    hd = (N_HEADS, HEAD_DIM)

    wl = w_in[0]
    cut = 2 * D_MODEL + 3 * attn_w
    w_all = jnp.concatenate(
        [wl[:, :cut], wl[:, cut + N_HEADS:], jnp.pad(wl[:, cut:cut + N_HEADS], ((0, 0), (0, LANES - N_HEADS)))],
        axis=1).astype(BF16)
    bf_pad = jnp.pad(b_f[0], (0, LANES - N_HEADS)).reshape(1, LANES)
    rg = (conv_w[0], conv_b[0].reshape(1, d), _pair_blocks(rg_wr[0]), _pair_blocks(rg_wi[0]),
          rg_br[0].reshape(1, d), rg_bi[0].reshape(1, d), rg_lambda[0].reshape(1, d))
    wa, wb, wo = (w[0].astype(BF16) for w in (w_proj_a, w_proj_b, w_o))
    wfi, wfo = w_ffn_in[0].astype(BF16), w_ffn_out[0].astype(BF16)
    n1, n2, nf = norm1_g[0].reshape(1, d), norm2_g[0].reshape(1, d), normf_g.reshape(1, d)

    mod = _ada(jnp.concatenate([c_prompt, c_sample], axis=0), ada_w[0], ada_b[0])
    mod_p = mod[:bsz].reshape(bsz, 1, 6 * d)
    mod_s = mod[bsz:].reshape(1, nsmp, 6 * d)

    xp = x_prompt.reshape(bsz * seq, d)
    ga, gb, lf, q_aug, k_aug, kt, vt, vtb, a_in, h_last_p, conv_p = _in_proj(
        xp, mod_p, seq, ROW_TILE, n1, w_all, bf_pad, rg)
    logf_p = lf[:, :N_HEADS].reshape(bsz, seq, N_HEADS)
    xs = x_sample.reshape(nsmp, d)
    xr_s, gr_s, ga_s, gb_s, lf_s, q_s, k_s, v_s = _in_proj(xs, mod_s, nsmp, nsmp, n1, w_all, bf_pad)
    logf_s = lf_s[:, :N_HEADS]
    a_in_s, h_s = _rglru_step(xr_s, gr_s, jnp.transpose(state_conv[0], (1, 0, 2)), state_rglru[0],
                              rg, past_len == 0)

    att, att_s = _fox_attention(
        q_aug, k_aug, vtb, bsz, seq,
        q_s.astype(F32).reshape(nsmp, *hd), k_s.reshape(nsmp, *hd), v_s.reshape(nsmp, *hd),
        logf_s.reshape(nsmp, N_HEADS, 1),
        jnp.transpose(cache_k, (0, 1, 3, 4, 2)), jnp.transpose(cache_v, (0, 1, 3, 4, 2)),
        jnp.transpose(cache_logf, (0, 1, 3, 2)), page_table)

    k_p = jnp.transpose(kt.reshape(1, bsz, *hd, seq), (0, 1, 4, 2, 3))
    v_p = jnp.transpose(vt.reshape(1, bsz, *hd, seq), (0, 1, 4, 2, 3))
    x1, h2 = _merge(xp, a_in, att, ga, gb, mod_p, seq, DENSE_TILE, n2, wa, wb, wo)
    y_p = _ffn(x1, h2, mod_p, seq, DENSE_TILE, nf, wfi, wfo).reshape(bsz, seq, d)

    x1_s, h2_s = _merge(xs, a_in_s, att_s.reshape(nsmp, d).astype(BF16), ga_s, gb_s, mod_s, nsmp,
                        nsmp, n2, wa, wb, wo)
    y_s = _ffn(x1_s, h2_s, mod_s, nsmp, nsmp, nf, wfi, wfo).reshape(nsmp, 1, d)
    conv_s = jnp.concatenate([state_conv[0][:, 1:], xr_s[:, None, :]], axis=1)

    return (y_p, y_s,
            k_p, v_p, logf_p[None],
            conv_p[None], h_last_p.reshape(1, bsz, d),
            k_s.reshape(1, nsmp, 1, *hd), v_s.reshape(1, nsmp, 1, *hd),
            logf_s.reshape(1, nsmp, 1, N_HEADS),
            conv_s[None], h_s[None])
```
